```python
import jax, jax.numpy as jnp
from jax import lax
import numpy as np

D_MODEL = 4096
BATCH = 8
SEQ = 2048
DEPTH = 1
DEC_BATCH = 16
DEC_SEQ = 32
PAST_LEN = 1024

CHUNK = 64
Q_BLOCK = 128
HEAD_DIM = 128
N_HEADS = D_MODEL // HEAD_DIM
H_FOX = N_HEADS // 2
H_SB = N_HEADS - H_FOX
W_FOX = H_FOX * HEAD_DIM
W_SB = H_SB * HEAD_DIM
D_MIX = W_FOX + W_SB
D_IN = 3 * W_FOX + H_FOX + 3 * W_SB
D_FF = ((8 * D_MODEL // 3 + 255) // 256) * 256
EPS = 1e-6
NEG_INF = -1e30
FORGET_BIAS_INIT = 2.0

kernel_name = "fox_stickbreak_hybrid_stream_step"


def _rmsnorm(x, g):
    x32 = x.astype(jnp.float32)
    y = x32 * lax.rsqrt(jnp.mean(x32 * x32, axis=-1, keepdims=True) + EPS)
    return (y * g.astype(jnp.float32)).astype(x.dtype)


def _project(h, w_in, b_forget):
    b, t = h.shape[:2]
    z = h @ w_in
    cuts = np.cumsum([W_FOX, W_FOX, W_FOX, H_FOX, W_SB, W_SB]).tolist()
    qf, kf, vf, fl, qs, ks, vs = jnp.split(z, cuts, axis=-1)
    r_fox = lambda a: a.reshape(b, t, H_FOX, HEAD_DIM)
    r_sb = lambda a: a.reshape(b, t, H_SB, HEAD_DIM)
    logf = jax.nn.log_sigmoid(fl.astype(jnp.float32) + b_forget.astype(jnp.float32))
    return r_fox(qf), r_fox(kf), r_fox(vf), logf, r_sb(qs), r_sb(ks), r_sb(vs)


def _fox_attend(q, k, v, cq, ck, qpos, kpos):
    s = jnp.einsum('bqhd,bkhd->bhqk', q, k).astype(jnp.float32) * (HEAD_DIM ** -0.5)
    cq32 = jnp.transpose(cq.astype(jnp.float32), (0, 2, 1))
    ck32 = jnp.transpose(ck.astype(jnp.float32), (0, 2, 1))
    s = s + (cq32[..., :, None] - ck32[..., None, :])
    mask = kpos[None, :] <= qpos[:, None]
    s = jnp.where(mask, s, NEG_INF)
    p = jax.nn.softmax(s, axis=-1)
    return jnp.einsum('bhqk,bkhd->bqhd', p.astype(v.dtype), v)


def _sb_attend(q, k, v, qpos, kpos):
    z = jnp.einsum('bqhd,bkhd->bhqk', q, k).astype(jnp.float32) * (HEAD_DIM ** -0.5)
    mask = kpos[None, :] < qpos[:, None]
    log_1mb = jnp.where(mask, jax.nn.log_sigmoid(-z), 0.0)
    between = lax.cumsum(log_1mb, axis=3, reverse=True) - log_1mb
    a = jnp.where(mask, jnp.exp(jax.nn.log_sigmoid(z) + between), 0.0)
    return jnp.einsum('bhqk,bkhd->bqhd', a.astype(v.dtype), v)


def _to_blocks(a):
    b, t = a.shape[:2]
    a = a.reshape((b, t // Q_BLOCK, Q_BLOCK) + a.shape[2:])
    return jnp.moveaxis(a, 1, 0)


def _from_blocks(a):
    a = jnp.moveaxis(a, 0, 1)
    return a.reshape((a.shape[0], a.shape[1] * a.shape[2]) + a.shape[3:])


def _merge_groups(o_fox, o_sb, g_fox, g_sb, w_out):
    b, t = o_fox.shape[:2]
    a = _rmsnorm(o_fox.reshape(b, t, W_FOX), g_fox)
    s = _rmsnorm(o_sb.reshape(b, t, W_SB), g_sb)
    return jnp.concatenate([a, s], axis=-1) @ w_out


def _swiglu(h, w_gate, w_up, w_down):
    return (jax.nn.silu(h @ w_gate) * (h @ w_up)) @ w_down


def setup_inputs(seed: int = 0) -> dict:
    key = jax.random.key(seed)
    ks = jax.random.split(key, 20)
    f32 = jnp.float32
    nrm = lambda k, shape, scale: jax.random.normal(k, shape, f32) * scale
    cache_kv_fox = (DEPTH, DEC_BATCH, PAST_LEN, H_FOX, HEAD_DIM)
    cache_kv_sb = (DEPTH, DEC_BATCH, PAST_LEN, H_SB, HEAD_DIM)
    return {
        "x_prompt": nrm(ks[0], (BATCH, SEQ, D_MODEL), 1.0),
        "x_sample": nrm(ks[1], (DEC_BATCH, DEC_SEQ, D_MODEL), 1.0),
        "cache_fox_k": nrm(ks[2], cache_kv_fox, 1.0),
        "cache_fox_v": nrm(ks[3], cache_kv_fox, 1.0),
        "cache_fox_logf": jax.nn.log_sigmoid(FORGET_BIAS_INIT + nrm(ks[4], (DEPTH, DEC_BATCH, PAST_LEN, H_FOX), 1.0)),
        "cache_sb_k": nrm(ks[5], cache_kv_sb, 1.0),
        "cache_sb_v": nrm(ks[6], cache_kv_sb, 1.0),
        "attn_norm": 1.0 + nrm(ks[7], (DEPTH, D_MODEL), 0.01),
        "w_in": nrm(ks[8], (DEPTH, D_MODEL, D_IN), D_MODEL ** -0.5),
        "b_forget": FORGET_BIAS_INIT + nrm(ks[9], (DEPTH, H_FOX), 0.1),
        "out_norm_fox": 1.0 + nrm(ks[10], (DEPTH, W_FOX), 0.01),
        "out_norm_sb": 1.0 + nrm(ks[11], (DEPTH, W_SB), 0.01),
        "w_out": nrm(ks[12], (DEPTH, D_MIX, D_MODEL), D_MIX ** -0.5),
        "ffn_norm": 1.0 + nrm(ks[13], (DEPTH, D_MODEL), 0.01),
        "w_gate": nrm(ks[14], (DEPTH, D_MODEL, D_FF), D_MODEL ** -0.5),
        "w_up": nrm(ks[15], (DEPTH, D_MODEL, D_FF), D_MODEL ** -0.5),
        "w_down": nrm(ks[16], (DEPTH, D_FF, D_MODEL), D_FF ** -0.5),
        "final_norm": 1.0 + nrm(ks[17], (D_MODEL,), 0.01),
    }


def reference(x_prompt, x_sample, cache_fox_k, cache_fox_v, cache_fox_logf, cache_sb_k, cache_sb_v,
              attn_norm, w_in, b_forget, out_norm_fox, out_norm_sb, w_out,
              ffn_norm, w_gate, w_up, w_down, final_norm):
    xp, xs = x_prompt, x_sample
    t_p = xp.shape[1]
    t_s = xs.shape[1]
    past = cache_fox_k.shape[2]
    pos_p = jnp.arange(t_p, dtype=jnp.int32)
    pos_all = jnp.arange(past + t_s, dtype=jnp.int32)
    pos_q_s = past + jnp.arange(t_s, dtype=jnp.int32)

    p_fk, p_fv, p_fl, p_sk, p_sv = [], [], [], [], []
    s_fk, s_fv, s_fl, s_sk, s_sv = [], [], [], [], []
    for l in range(DEPTH):
        hp = _rmsnorm(xp, attn_norm[l])
        qf, kf, vf, lf, qsb, ksb, vsb = _project(hp, w_in[l], b_forget[l])
        cf = jnp.cumsum(lf, axis=1)
        o_fox = _from_blocks(lax.map(
            lambda blk: _fox_attend(blk[0], kf, vf, blk[1], cf, blk[2], pos_p),
            (_to_blocks(qf), _to_blocks(cf), pos_p.reshape(-1, Q_BLOCK))))
        o_sb = _from_blocks(lax.map(
            lambda blk: _sb_attend(blk[0], ksb, vsb, blk[1], pos_p),
            (_to_blocks(qsb), pos_p.reshape(-1, Q_BLOCK))))
        xp = xp + _merge_groups(o_fox, o_sb, out_norm_fox[l], out_norm_sb[l], w_out[l])
        xp = xp + _swiglu(_rmsnorm(xp, ffn_norm[l]), w_gate[l], w_up[l], w_down[l])
        p_fk.append(kf); p_fv.append(vf); p_fl.append(lf); p_sk.append(ksb); p_sv.append(vsb)

        hs = _rmsnorm(xs, attn_norm[l])
        qf2, kf2, vf2, lf2, qsb2, ksb2, vsb2 = _project(hs, w_in[l], b_forget[l])
        kf_all = jnp.concatenate([cache_fox_k[l], kf2], axis=1)
        vf_all = jnp.concatenate([cache_fox_v[l], vf2], axis=1)
        cf_all = jnp.cumsum(jnp.concatenate([cache_fox_logf[l].astype(jnp.float32), lf2], axis=1), axis=1)
        ksb_all = jnp.concatenate([cache_sb_k[l], ksb2], axis=1)
        vsb_all = jnp.concatenate([cache_sb_v[l], vsb2], axis=1)
        o_fox2 = _fox_attend(qf2, kf_all, vf_all, cf_all[:, past:], cf_all, pos_q_s, pos_all)
        o_sb2 = _sb_attend(qsb2, ksb_all, vsb_all, pos_q_s, pos_all)
        xs = xs + _merge_groups(o_fox2, o_sb2, out_norm_fox[l], out_norm_sb[l], w_out[l])
        xs = xs + _swiglu(_rmsnorm(xs, ffn_norm[l]), w_gate[l], w_up[l], w_down[l])
        s_fk.append(kf2); s_fv.append(vf2); s_fl.append(lf2); s_sk.append(ksb2); s_sv.append(vsb2)

    y_prompt = _rmsnorm(xp, final_norm)
    y_sample = _rmsnorm(xs, final_norm)
    return (y_prompt, y_sample,
            jnp.stack(p_fk), jnp.stack(p_fv), jnp.stack(p_fl), jnp.stack(p_sk), jnp.stack(p_sv),
            jnp.stack(s_fk), jnp.stack(s_fv), jnp.stack(s_fl), jnp.stack(s_sk), jnp.stack(s_sv))
```

```python
import functools

import jax
import jax.numpy as jnp
from jax import lax
from jax.experimental import pallas as pl
from jax.experimental.pallas import tpu as pltpu

EPS = 1e-6
NEG_INF = -1e30
HEAD_DIM = 128
LANES = 128
VMEM_LIMIT_BYTES = 56 * 1024 * 1024

F32 = jnp.float32
BF16 = jnp.bfloat16


def _params(*sem):
    return pltpu.CompilerParams(dimension_semantics=sem, vmem_limit_bytes=VMEM_LIMIT_BYTES)


def _pick(n, candidates):
    for c in candidates:
        if n % c == 0:
            return c
    return n


def _softplus(z):
    return jnp.maximum(z, 0.0) + jnp.log1p(jnp.exp(-jnp.abs(z)))


def _dot(a, b):
    return jnp.dot(a, b, preferred_element_type=F32)


def _dot_nt(a, b):
    return lax.dot_general(a, b, (((1,), (1,)), ((), ())), preferred_element_type=F32)


def _rmsnorm_kernel(x_ref, g_ref, o_ref):
    x = x_ref[...]
    r = lax.rsqrt(jnp.mean(x * x, axis=-1, keepdims=True) + EPS)
    o_ref[...] = (x * r * g_ref[...]).astype(o_ref.dtype)


def _rmsnorm(x, g, out_dtype):
    m, d = x.shape
    tm = _pick(m, (256, 128, 64, 32, 16, 8))
    return pl.pallas_call(
        _rmsnorm_kernel,
        grid=(m // tm,),
        in_specs=[pl.BlockSpec((tm, d), lambda i: (i, 0)),
                  pl.BlockSpec((1, d), lambda i: (0, 0))],
        out_specs=pl.BlockSpec((tm, d), lambda i: (i, 0)),
        out_shape=jax.ShapeDtypeStruct((m, d), out_dtype),
        compiler_params=_params("parallel"),
        name="rmsnorm",
    )(x, g.reshape(1, d).astype(F32))


def _merge_norm_kernel(a_ref, b_ref, ga_ref, gb_ref, o_ref):
    wa = a_ref.shape[1]
    for ref, g_ref, lo in ((a_ref, ga_ref, 0), (b_ref, gb_ref, wa)):
        x = ref[...]
        r = lax.rsqrt(jnp.mean(x * x, axis=-1, keepdims=True) + EPS)
        o_ref[:, lo:lo + x.shape[1]] = (x * r * g_ref[...]).astype(o_ref.dtype)


def _merge_norm(o_fox, o_sb, g_fox, g_sb):
    m, wa = o_fox.shape
    wb = o_sb.shape[1]
    tm = _pick(m, (256, 128, 64, 32, 16, 8))
    return pl.pallas_call(
        _merge_norm_kernel,
        grid=(m // tm,),
        in_specs=[pl.BlockSpec((tm, wa), lambda i: (i, 0)),
                  pl.BlockSpec((tm, wb), lambda i: (i, 0)),
                  pl.BlockSpec((1, wa), lambda i: (0, 0)),
                  pl.BlockSpec((1, wb), lambda i: (0, 0))],
        out_specs=pl.BlockSpec((tm, wa + wb), lambda i: (i, 0)),
        out_shape=jax.ShapeDtypeStruct((m, wa + wb), BF16),
        compiler_params=_params("parallel"),
        name="merge_norm",
    )(o_fox, o_sb, g_fox.reshape(1, wa).astype(F32), g_sb.reshape(1, wb).astype(F32))


def _proj_q_kernel(x_ref, w_ref, o_ref, *, scale):
    o_ref[...] = (_dot(x_ref[...], w_ref[...]) * scale).astype(o_ref.dtype)


def _proj_kv_kernel(x_ref, w_ref, o32_ref, o16_ref):
    acc = _dot(x_ref[...], w_ref[...])
    o32_ref[...] = acc
    o16_ref[...] = acc.astype(o16_ref.dtype)


def _proj_logf_kernel(x_ref, w_ref, b_ref, o_ref):
    o_ref[...] = -_softplus(-(_dot(x_ref[...], w_ref[...]) + b_ref[...]))


def _proj_resid_kernel(x_ref, w_ref, r_ref, o_ref):
    o_ref[...] = r_ref[...] + _dot(x_ref[...], w_ref[...])


def _proj_swiglu_kernel(x_ref, wg_ref, wu_ref, o_ref):
    x = x_ref[...]
    g = _dot(x, wg_ref[...])
    u = _dot(x, wu_ref[...])
    o_ref[...] = (g * jax.nn.sigmoid(g) * u).astype(o_ref.dtype)


def _matmul_call(kernel, x, weights, extra_row_inputs, extra_tile_inputs, out_dtypes, tn_candidates, name,
                 tm_candidates=(1024, 512, 256, 128, 64, 32, 16, 8)):
    m, k = x.shape
    n = weights[0].shape[1]
    tm = _pick(m, tm_candidates)
    tn = _pick(n, tn_candidates)
    tile = pl.BlockSpec((tm, tn), lambda i, j: (i, j))
    in_specs = [pl.BlockSpec((tm, k), lambda i, j: (i, 0))]
    in_specs += [pl.BlockSpec((k, tn), lambda i, j: (0, j)) for _ in weights]
    in_specs += [pl.BlockSpec((1, tn), lambda i, j: (0, j)) for _ in extra_row_inputs]
    in_specs += [tile for _ in extra_tile_inputs]
    outs = pl.pallas_call(
        kernel,
        grid=(m // tm, n // tn),
        in_specs=in_specs,
        out_specs=[tile for _ in out_dtypes],
        out_shape=[jax.ShapeDtypeStruct((m, n), dt) for dt in out_dtypes],
        compiler_params=_params("parallel", "arbitrary"),
        name=name,
    )(x, *weights, *extra_row_inputs, *extra_tile_inputs)
    return outs


def _split3(x):
    hi = x.astype(BF16)
    r = x - hi.astype(F32)
    mid = r.astype(BF16)
    lo = (r - mid.astype(F32)).astype(BF16)
    return hi, mid, lo


def _cumsum_kernel(x_ref, init_ref, tril_ref, ct_ref, last_ref, carry_sc):
    t = pl.program_id(1)

    @pl.when(t == 0)
    def _():
        carry_sc[...] = init_ref[...]

    parts = jnp.concatenate(_split3(x_ref[...]), axis=1)
    s = _dot(tril_ref[...], parts)
    cum = s[:, :LANES] + s[:, LANES:2 * LANES] + s[:, 2 * LANES:] + carry_sc[...]
    carry_sc[...] = cum[-1:, :]
    ct_ref[...] = cum.T
    last_ref[...] = cum[-1:, :]


def _cumsum_t(x, init):
    b, t, _ = x.shape
    tt = _pick(t, (512, 256, 128))
    idx = jnp.arange(tt)
    tril = (idx[:, None] >= idx[None, :]).astype(BF16)
    return pl.pallas_call(
        _cumsum_kernel,
        grid=(b, t // tt),
        in_specs=[pl.BlockSpec((None, tt, LANES), lambda i, j: (i, j, 0)),
                  pl.BlockSpec((None, 1, LANES), lambda i, j: (i, 0, 0)),
                  pl.BlockSpec((tt, tt), lambda i, j: (0, 0))],
        out_specs=[pl.BlockSpec((None, LANES, tt), lambda i, j: (i, 0, j)),
                   pl.BlockSpec((None, 1, LANES), lambda i, j: (i, 0, 0))],
        out_shape=[jax.ShapeDtypeStruct((b, LANES, t), F32),
                   jax.ShapeDtypeStruct((b, 1, LANES), F32)],
        scratch_shapes=[pltpu.VMEM((1, LANES), F32)],
        compiler_params=_params("parallel", "arbitrary"),
        name="cumsum_t",
    )(x, init, tril)


def _fox_prompt_kernel(q_ref, k_ref, v_ref, ck_ref, o_ref, m_sc, l_sc, acc_sc, *, tq):
    qi = pl.program_id(2)
    q = q_ref[...]
    m_sc[...] = jnp.full(m_sc.shape, -jnp.inf, F32)
    l_sc[...] = jnp.zeros(l_sc.shape, F32)
    acc_sc[...] = jnp.zeros(acc_sc.shape, F32)

    def block(kb, diagonal):
        start = pl.multiple_of(kb * tq, tq)
        k = k_ref[pl.ds(start, tq), :]
        v = v_ref[pl.ds(start, tq), :]
        s = _dot_nt(q, k) - ck_ref[pl.ds(kb, 1), :]
        if diagonal:
            row = lax.broadcasted_iota(jnp.int32, s.shape, 0)
            col = lax.broadcasted_iota(jnp.int32, s.shape, 1)
            s = jnp.where(col <= row, s, NEG_INF)
        m_prev = m_sc[...]
        m_new = jnp.maximum(m_prev, jnp.max(s, axis=-1, keepdims=True))
        alpha = jnp.exp(m_prev - m_new)
        p = jnp.exp(s - m_new)
        l_sc[...] = alpha * l_sc[...] + jnp.sum(p, axis=-1, keepdims=True)
        acc_sc[...] = alpha * acc_sc[...] + _dot(p.astype(BF16), v)
        m_sc[...] = m_new

    def body(kb, carry):
        block(kb, False)
        return carry

    lax.fori_loop(0, qi, body, 0)
    block(qi, True)
    o_ref[...] = acc_sc[...] / l_sc[...]


def _fox_prompt(q, k, v, ck, b, t, h):
    tq = _pick(t, (512, 256, 128))
    nq = t // tq
    return pl.pallas_call(
        functools.partial(_fox_prompt_kernel, tq=tq),
        grid=(b, h, nq),
        in_specs=[pl.BlockSpec((tq, HEAD_DIM), lambda i, j, l: (i * nq + l, j)),
                  pl.BlockSpec((t, HEAD_DIM), lambda i, j, l: (i, j)),
                  pl.BlockSpec((t, HEAD_DIM), lambda i, j, l: (i, j)),
                  pl.BlockSpec((None, None, nq, tq), lambda i, j, l: (i, j, 0, 0))],
        out_specs=pl.BlockSpec((tq, HEAD_DIM), lambda i, j, l: (i * nq + l, j)),
        out_shape=jax.ShapeDtypeStruct((b * t, h * HEAD_DIM), F32),
        scratch_shapes=[pltpu.VMEM((tq, 1), F32), pltpu.VMEM((tq, 1), F32), pltpu.VMEM((tq, HEAD_DIM), F32)],
        compiler_params=_params("parallel", "parallel", "arbitrary"),
        name="fox_prompt",
    )(q, k, v, ck)


def _split2(x):
    hi = x.astype(BF16)
    lo = (x - hi.astype(F32)).astype(BF16)
    return jnp.concatenate([hi, lo], axis=1)


def _sb_block(z, v, tri2, carry, valid):
    sp = _softplus(z)
    if valid is not None:
        sp = jnp.where(valid, sp, 0.0)
    between = _dot(_split2(sp), tri2)
    a = jnp.exp(z - sp - between - carry)
    if valid is not None:
        a = jnp.where(valid, a, 0.0)
    return _dot(a.astype(BF16), v), carry + jnp.sum(sp, axis=-1, keepdims=True)


def _sb_prompt_kernel(q_ref, k_ref, v_ref, tri2_ref, o_ref, carry_sc, acc_sc, *, tq):
    qi = pl.program_id(2)
    q = q_ref[...]
    tri2 = tri2_ref[...]

    def block(kb, diagonal, first):
        start = pl.multiple_of(kb * tq, tq)
        k = k_ref[pl.ds(start, tq), :]
        v = v_ref[pl.ds(start, tq), :]
        z = _dot_nt(q, k)
        valid = None
        if diagonal:
            row = lax.broadcasted_iota(jnp.int32, z.shape, 0)
            col = lax.broadcasted_iota(jnp.int32, z.shape, 1)
            valid = col < row
        carry = jnp.zeros(carry_sc.shape, F32) if first else carry_sc[...]
        pv, carry = _sb_block(z, v, tri2, carry, valid)
        carry_sc[...] = carry
        acc_sc[...] = pv if first else acc_sc[...] + pv

    block(qi, True, True)

    def body(i, c):
        block(qi - 1 - i, False, False)
        return c

    lax.fori_loop(0, qi, body, 0)
    o_ref[...] = acc_sc[...]


def _tri2(tk):
    idx = jnp.arange(tk)
    tri = (idx[:, None] > idx[None, :]).astype(BF16)
    return jnp.concatenate([tri, tri], axis=0)


def _sb_prompt(q, k, v, b, t, h):
    tq = _pick(t, (512, 256, 128))
    nq = t // tq
    return pl.pallas_call(
        functools.partial(_sb_prompt_kernel, tq=tq),
        grid=(b, h, nq),
        in_specs=[pl.BlockSpec((tq, HEAD_DIM), lambda i, j, l: (i * nq + l, j)),
                  pl.BlockSpec((t, HEAD_DIM), lambda i, j, l: (i, j)),
                  pl.BlockSpec((t, HEAD_DIM), lambda i, j, l: (i, j)),
                  pl.BlockSpec((2 * tq, tq), lambda i, j, l: (0, 0))],
        out_specs=pl.BlockSpec((tq, HEAD_DIM), lambda i, j, l: (i * nq + l, j)),
        out_shape=jax.ShapeDtypeStruct((b * t, h * HEAD_DIM), F32),
        scratch_shapes=[pltpu.VMEM((tq, 1), F32), pltpu.VMEM((tq, HEAD_DIM), F32)],
        compiler_params=_params("parallel", "parallel", "arbitrary"),
        name="sb_prompt",
    )(q, k, v, _tri2(tq))


def _fox_sample_kernel(q_ref, kc_ref, vc_ref, kn_ref, vn_ref, ckc_ref, ckn_ref, o_ref):
    q = q_ref[...]
    s1 = _dot_nt(q, kc_ref[...].astype(BF16)) - ckc_ref[...]
    s2 = _dot_nt(q, kn_ref[...]) - ckn_ref[...]
    row = lax.broadcasted_iota(jnp.int32, s2.shape, 0)
    col = lax.broadcasted_iota(jnp.int32, s2.shape, 1)
    s2 = jnp.where(col <= row, s2, NEG_INF)
    m = jnp.maximum(jnp.max(s1, axis=-1, keepdims=True), jnp.max(s2, axis=-1, keepdims=True))
    p1 = jnp.exp(s1 - m)
    p2 = jnp.exp(s2 - m)
    l = jnp.sum(p1, axis=-1, keepdims=True) + jnp.sum(p2, axis=-1, keepdims=True)
    acc = _dot(p1.astype(BF16), vc_ref[...].astype(BF16)) + _dot(p2.astype(BF16), vn_ref[...])
    o_ref[...] = acc / l


def _sb_sample_kernel(q_ref, kc_ref, vc_ref, kn_ref, vn_ref, tric_ref, trin_ref, o_ref, *, tc):
    q = q_ref[...]
    z2 = _dot_nt(q, kn_ref[...])
    row = lax.broadcasted_iota(jnp.int32, z2.shape, 0)
    col = lax.broadcasted_iota(jnp.int32, z2.shape, 1)
    acc, carry = _sb_block(z2, vn_ref[...], trin_ref[...], jnp.zeros((q.shape[0], 1), F32), col < row)
    tric = tric_ref[...]
    past = kc_ref.shape[0]
    for c in reversed(range(past // tc)):
        k = kc_ref[c * tc:(c + 1) * tc, :].astype(BF16)
        v = vc_ref[c * tc:(c + 1) * tc, :].astype(BF16)
        pv, carry = _sb_block(_dot_nt(q, k), v, tric, carry, None)
        acc = acc + pv
    o_ref[...] = acc


def _sample_specs(ts, past, h):
    new = pl.BlockSpec((ts, HEAD_DIM), lambda i, j: (i, j))
    cache = pl.BlockSpec((None, past, HEAD_DIM), lambda i, j: (i, 0, j))
    return new, cache


def _fox_sample(q, kc, vc, kn, vn, ckc, ckn, b, ts, past, h):
    new, cache = _sample_specs(ts, past, h)
    return pl.pallas_call(
        _fox_sample_kernel,
        grid=(b, h),
        in_specs=[new, cache, cache, new, new,
                  pl.BlockSpec((None, None, 1, past), lambda i, j: (i, j, 0, 0)),
                  pl.BlockSpec((None, None, 1, ts), lambda i, j: (i, j, 0, 0))],
        out_specs=new,
        out_shape=jax.ShapeDtypeStruct((b * ts, h * HEAD_DIM), F32),
        compiler_params=_params("parallel", "parallel"),
        name="fox_sample",
    )(q, kc, vc, kn, vn, ckc, ckn)


def _sb_sample(q, kc, vc, kn, vn, b, ts, past, h):
    new, cache = _sample_specs(ts, past, h)
    tc = _pick(past, (256, 128))
    return pl.pallas_call(
        functools.partial(_sb_sample_kernel, tc=tc),
        grid=(b, h),
        in_specs=[new, cache, cache, new, new,
                  pl.BlockSpec((2 * tc, tc), lambda i, j: (0, 0)),
                  pl.BlockSpec((2 * ts, ts), lambda i, j: (0, 0))],
        out_specs=new,
        out_shape=jax.ShapeDtypeStruct((b * ts, h * HEAD_DIM), F32),
        compiler_params=_params("parallel", "parallel"),
        name="sb_sample",
    )(q, kc, vc, kn, vn, _tri2(tc), _tri2(ts))


def _layer(x, b, t, w, caches):
    hf = w["b_forget"].shape[0]
    w_fox = hf * HEAD_DIM
    hs = w["w_sb"][0].shape[1] // HEAD_DIM
    scale = HEAD_DIM ** -0.5
    big = (1024, 512, 256, 128)

    hn = _rmsnorm(x, w["attn_norm"], BF16)
    qkv = {}
    for grp in ("fox", "sb"):
        wq, wk, wv = w["w_" + grp]
        (q,) = _matmul_call(functools.partial(_proj_q_kernel, scale=scale), hn, [wq], [], [], [BF16], big,
                            "proj_q_" + grp)
        k32, k16 = _matmul_call(_proj_kv_kernel, hn, [wk], [], [], [F32, BF16], big, "proj_k_" + grp)
        v32, v16 = _matmul_call(_proj_kv_kernel, hn, [wv], [], [], [F32, BF16], big, "proj_v_" + grp)
        qkv[grp] = (q, k32, k16, v32, v16)
    (logf_pad,) = _matmul_call(_proj_logf_kernel, hn, [w["w_fl"]], [w["b_fl"]], [], [F32], (LANES,), "proj_logf")

    qf, kf32, kf16, vf32, vf16 = qkv["fox"]
    qs, ks32, ks16, vs32, vs16 = qkv["sb"]
    zeros_init = jnp.zeros((b, 1, LANES), F32)
    if caches is None:
        ct, _ = _cumsum_t(logf_pad.reshape(b, t, LANES), zeros_init)
        tq = _pick(t, (512, 256, 128))
        ck = ct[:, :hf, :].reshape(b, hf, t // tq, tq)
        o_fox = _fox_prompt(qf, kf16, vf16, ck, b, t, hf)
        o_sb = _sb_prompt(qs, ks16, vs16, b, t, hs)
    else:
        c_fk, c_fv, c_fl, c_sk, c_sv = caches
        past = c_fk.shape[1]
        c_fl_pad = jnp.pad(c_fl.astype(F32), ((0, 0), (0, 0), (0, LANES - hf)))
        ct_c, last = _cumsum_t(c_fl_pad, zeros_init)
        ct_n, _ = _cumsum_t(logf_pad.reshape(b, t, LANES), last)
        ckc = ct_c[:, :hf, :].reshape(b, hf, 1, past)
        ckn = ct_n[:, :hf, :].reshape(b, hf, 1, t)
        o_fox = _fox_sample(qf, c_fk.reshape(b, past, w_fox), c_fv.reshape(b, past, w_fox), kf16, vf16,
                            ckc, ckn, b, t, past, hf)
        o_sb = _sb_sample(qs, c_sk.reshape(b, past, hs * HEAD_DIM), c_sv.reshape(b, past, hs * HEAD_DIM),
                          ks16, vs16, b, t, past, hs)

    a = _merge_norm(o_fox, o_sb, w["out_norm_fox"], w["out_norm_sb"])
    (x1,) = _matmul_call(_proj_resid_kernel, a, [w["w_out"]], [], [x], [F32], big, "proj_out")
    h2 = _rmsnorm(x1, w["ffn_norm"], BF16)
    (act,) = _matmul_call(_proj_swiglu_kernel, h2, [w["w_gate"], w["w_up"]], [], [], [BF16], (256, 128), "ffn_up")
    (x2,) = _matmul_call(_proj_resid_kernel, act, [w["w_down"]], [], [x1], [F32], (256, 128), "ffn_down",
                         tm_candidates=(512, 256, 128, 64, 32, 16, 8))
    return x2, kf32, vf32, logf_pad[:, :hf], ks32, vs32


def kernel(x_prompt, x_sample, cache_fox_k, cache_fox_v, cache_fox_logf, cache_sb_k, cache_sb_v, attn_norm, w_in,
           b_forget, out_norm_fox, out_norm_sb, w_out, ffn_norm, w_gate, w_up, w_down, final_norm):
    bp, tp, d = x_prompt.shape
    bs, ts, _ = x_sample.shape
    depth = w_in.shape[0]
    hf = cache_fox_k.shape[3]
    hs = cache_sb_k.shape[3]
    w_fox = hf * HEAD_DIM
    w_sb = hs * HEAD_DIM

    xp = x_prompt.reshape(bp * tp, d)
    xs = x_sample.reshape(bs * ts, d)
    outs_p, outs_s = [], []
    for l in range(depth):
        wl = w_in[l]
        cuts = [0, w_fox, 2 * w_fox, 3 * w_fox, 3 * w_fox + hf, 3 * w_fox + hf + w_sb, 3 * w_fox + hf + 2 * w_sb,
                3 * w_fox + hf + 3 * w_sb]
        seg = [wl[:, cuts[i]:cuts[i + 1]] for i in range(7)]
        w = {
            "attn_norm": attn_norm[l], "ffn_norm": ffn_norm[l],
            "out_norm_fox": out_norm_fox[l], "out_norm_sb": out_norm_sb[l],
            "w_fox": [s.astype(BF16) for s in seg[0:3]],
            "w_sb": [s.astype(BF16) for s in seg[4:7]],
            "w_fl": jnp.pad(seg[3], ((0, 0), (0, LANES - hf))).astype(BF16),
            "b_fl": jnp.pad(b_forget[l].astype(F32), (0, LANES - hf)).reshape(1, LANES),
            "b_forget": b_forget[l],
            "w_out": w_out[l].astype(BF16), "w_gate": w_gate[l].astype(BF16), "w_up": w_up[l].astype(BF16),
            "w_down": w_down[l].astype(BF16),
        }
        xp, *rest_p = _layer(xp, bp, tp, w, None)
        caches = (cache_fox_k[l], cache_fox_v[l], cache_fox_logf[l], cache_sb_k[l], cache_sb_v[l])
        xs, *rest_s = _layer(xs, bs, ts, w, caches)
        outs_p.append(rest_p)
        outs_s.append(rest_s)

    y_prompt = _rmsnorm(xp, final_norm, F32).reshape(bp, tp, d)
    y_sample = _rmsnorm(xs, final_norm, F32).reshape(bs, ts, d)

    def stack(outs, idx, shape):
        return jnp.stack([o[idx].reshape(shape) for o in outs])

    res = [y_prompt, y_sample]
    for outs, b, t in ((outs_p, bp, tp), (outs_s, bs, ts)):
        res += [stack(outs, 0, (b, t, hf, HEAD_DIM)), stack(outs, 1, (b, t, hf, HEAD_DIM)),
                stack(outs, 2, (b, t, hf)),
                stack(outs, 3, (b, t, hs, HEAD_DIM)), stack(outs, 4, (b, t, hs, HEAD_DIM))]
    return tuple(res)
```

```python
import functools
import math

import jax
import jax.numpy as jnp
from jax import lax
from jax.experimental import pallas as pl
from jax.experimental.pallas import tpu as pltpu

EPS = 1e-6
NEG_INF = -1e30
HEAD_DIM = 128
LANES = 128
LOG2E = math.log2(math.e)
VMEM_LIMIT_BYTES = 56 * 1024 * 1024
SCORE_BUFFERS = 3

F32 = jnp.float32
BF16 = jnp.bfloat16


def _params(*sem):
    return pltpu.CompilerParams(dimension_semantics=sem, vmem_limit_bytes=VMEM_LIMIT_BYTES)


def _pick(n, candidates):
    for c in candidates:
        if n % c == 0:
            return c
    return n


def _softplus(z):
    return jnp.maximum(z, 0.0) + jnp.log1p(jnp.exp(-jnp.abs(z)))


def _softplus2(z):
    return jnp.maximum(z, 0.0) + jnp.log2(1.0 + jnp.exp2(-jnp.abs(z)))


def _dot(a, b):
    return jnp.dot(a, b, preferred_element_type=F32)


def _dot_nt(a, b):
    return lax.dot_general(a, b, (((1,), (1,)), ((), ())), preferred_element_type=F32)


def _split2(x):
    hi = x.astype(BF16)
    lo = (x - hi.astype(F32)).astype(BF16)
    return jnp.concatenate([hi, lo], axis=1)


def _split3(x):
    hi = x.astype(BF16)
    r = x - hi.astype(F32)
    mid = r.astype(BF16)
    lo = (r - mid.astype(F32)).astype(BF16)
    return hi, mid, lo


def _rmsnorm_kernel(x_ref, g_ref, o_ref):
    x = x_ref[...]
    r = lax.rsqrt(jnp.mean(x * x, axis=-1, keepdims=True) + EPS)
    o_ref[...] = (x * r * g_ref[...]).astype(o_ref.dtype)


def _rmsnorm(x, g, out_dtype):
    m, d = x.shape
    tm = _pick(m, (256, 128, 64, 32, 16, 8))
    return pl.pallas_call(
        _rmsnorm_kernel,
        grid=(m // tm,),
        in_specs=[pl.BlockSpec((tm, d), lambda i: (i, 0)),
                  pl.BlockSpec((1, d), lambda i: (0, 0))],
        out_specs=pl.BlockSpec((tm, d), lambda i: (i, 0)),
        out_shape=jax.ShapeDtypeStruct((m, d), out_dtype),
        compiler_params=_params("parallel"),
        name="rmsnorm",
    )(x, g.reshape(1, d).astype(F32))


def _merge_norm_kernel(a_ref, b_ref, ga_ref, gb_ref, o_ref):
    wa = a_ref.shape[1]
    for ref, g_ref, lo in ((a_ref, ga_ref, 0), (b_ref, gb_ref, wa)):
        x = ref[...]
        r = lax.rsqrt(jnp.mean(x * x, axis=-1, keepdims=True) + EPS)
        o_ref[:, lo:lo + x.shape[1]] = (x * r * g_ref[...]).astype(o_ref.dtype)


def _merge_norm(o_fox, o_sb, g_fox, g_sb):
    m, wa = o_fox.shape
    wb = o_sb.shape[1]
    tm = _pick(m, (256, 128, 64, 32, 16, 8))
    return pl.pallas_call(
        _merge_norm_kernel,
        grid=(m // tm,),
        in_specs=[pl.BlockSpec((tm, wa), lambda i: (i, 0)),
                  pl.BlockSpec((tm, wb), lambda i: (i, 0)),
                  pl.BlockSpec((1, wa), lambda i: (0, 0)),
                  pl.BlockSpec((1, wb), lambda i: (0, 0))],
        out_specs=pl.BlockSpec((tm, wa + wb), lambda i: (i, 0)),
        out_shape=jax.ShapeDtypeStruct((m, wa + wb), BF16),
        compiler_params=_params("parallel"),
        name="merge_norm",
    )(o_fox, o_sb, g_fox.reshape(1, wa).astype(F32), g_sb.reshape(1, wb).astype(F32))


def _proj_q_kernel(x_ref, w_ref, o_ref, *, scale):
    o_ref[...] = (_dot(x_ref[...], w_ref[...]) * scale).astype(o_ref.dtype)


def _proj_f32_kernel(x_ref, w_ref, o_ref):
    o_ref[...] = _dot(x_ref[...], w_ref[...])


def _proj_logf_kernel(x_ref, w_ref, b_ref, o_ref):
    o_ref[...] = -_softplus(-(_dot(x_ref[...], w_ref[...]) + b_ref[...]))


def _proj_resid_kernel(x_ref, w_ref, r_ref, o_ref):
    o_ref[...] = r_ref[...] + _dot(x_ref[...], w_ref[...])


def _proj_swiglu_kernel(x_ref, wg_ref, wu_ref, o_ref):
    x = x_ref[...]
    g = _dot(x, wg_ref[...])
    u = _dot(x, wu_ref[...])
    o_ref[...] = (g * jax.nn.sigmoid(g) * u).astype(o_ref.dtype)


def _matmul_call(kernel, x, weights, extra_row_inputs, extra_tile_inputs, out_dtypes, tn_candidates, name,
                 tm_candidates=(1024, 512, 256, 128, 64, 32, 16, 8)):
    m, k = x.shape
    n = weights[0].shape[1]
    tm = _pick(m, tm_candidates)
    tn = _pick(n, tn_candidates)
    tile = pl.BlockSpec((tm, tn), lambda i, j: (i, j))
    in_specs = [pl.BlockSpec((tm, k), lambda i, j: (i, 0))]
    in_specs += [pl.BlockSpec((k, tn), lambda i, j: (0, j)) for _ in weights]
    in_specs += [pl.BlockSpec((1, tn), lambda i, j: (0, j)) for _ in extra_row_inputs]
    in_specs += [tile for _ in extra_tile_inputs]
    outs = pl.pallas_call(
        kernel,
        grid=(m // tm, n // tn),
        in_specs=in_specs,
        out_specs=[tile for _ in out_dtypes],
        out_shape=[jax.ShapeDtypeStruct((m, n), dt) for dt in out_dtypes],
        compiler_params=_params("parallel", "arbitrary"),
        name=name,
    )(x, *weights, *extra_row_inputs, *extra_tile_inputs)
    return outs


def _cumsum_kernel(x_ref, init_ref, tril_ref, ct_ref, last_ref, carry_sc):
    t = pl.program_id(1)

    @pl.when(t == 0)
    def _():
        carry_sc[...] = init_ref[...]

    parts = jnp.concatenate(_split3(x_ref[...]), axis=1)
    s = _dot(tril_ref[...], parts)
    cum = s[:, :LANES] + s[:, LANES:2 * LANES] + s[:, 2 * LANES:] + carry_sc[...]
    carry_sc[...] = cum[-1:, :]
    ct_ref[...] = cum.T
    last_ref[...] = cum[-1:, :]


def _cumsum_t(x, init):
    b, t, _ = x.shape
    tt = _pick(t, (512, 256, 128))
    idx = jnp.arange(tt)
    tril = (idx[:, None] >= idx[None, :]).astype(BF16)
    return pl.pallas_call(
        _cumsum_kernel,
        grid=(b, t // tt),
        in_specs=[pl.BlockSpec((None, tt, LANES), lambda i, j: (i, j, 0)),
                  pl.BlockSpec((None, 1, LANES), lambda i, j: (i, 0, 0)),
                  pl.BlockSpec((tt, tt), lambda i, j: (0, 0))],
        out_specs=[pl.BlockSpec((None, LANES, tt), lambda i, j: (i, 0, j)),
                   pl.BlockSpec((None, 1, LANES), lambda i, j: (i, 0, 0))],
        out_shape=[jax.ShapeDtypeStruct((b, LANES, t), F32),
                   jax.ShapeDtypeStruct((b, 1, LANES), F32)],
        scratch_shapes=[pltpu.VMEM((1, LANES), F32)],
        compiler_params=_params("parallel", "arbitrary"),
        name="cumsum_t",
    )(x, init, tril)


def _fox_prompt_kernel(q_ref, k_ref, v_ref, ck_ref, o_ref, kaug_sc, vaug_sc, m_sc, acc_sc, s_sc, *, tq, tk):
    t = k_ref.shape[0]
    nq = t // tq
    nbuf = s_sc.shape[0]
    hi, mid, lo = _split3(ck_ref[...] * LOG2E)
    rows = lax.broadcasted_iota(jnp.int32, (HEAD_DIM, t), 0)
    parts = jnp.where(rows == 0, hi.astype(F32),
                      jnp.where(rows == 1, mid.astype(F32), jnp.where(rows == 2, lo.astype(F32), 0.0)))
    kaug_sc[:, :HEAD_DIM] = k_ref[...].astype(BF16)
    kaug_sc[:, HEAD_DIM:] = parts.T.astype(BF16)
    lane = lax.broadcasted_iota(jnp.int32, (t, HEAD_DIM), 1)
    vaug_sc[:, :HEAD_DIM] = v_ref[...].astype(BF16)
    vaug_sc[:, HEAD_DIM:] = jnp.where(lane == 0, 1.0, 0.0).astype(BF16)

    lane_q = lax.broadcasted_iota(jnp.int32, (tq, HEAD_DIM), 1)
    q_extra = jnp.where(lane_q < 3, -1.0, 0.0).astype(BF16)
    nc = tk // LANES
    blocks = [(qi, kb) for qi in range(nq) for kb in range((qi * tq) // tk + 1)]

    def scores(n):
        qi, kb = blocks[n]
        qa = jnp.concatenate([q_ref[qi * tq:(qi + 1) * tq, :], q_extra], axis=1)
        s_sc[n % nbuf] = _dot_nt(qa, kaug_sc[kb * tk:(kb + 1) * tk, :])

    def softmax_pv(n):
        qi, kb = blocks[n]
        first = kb == 0
        last = kb == (qi * tq) // tk
        s = s_sc[n % nbuf]
        if last:
            row = qi * tq + lax.broadcasted_iota(jnp.int32, s.shape, 0)
            col = kb * tk + lax.broadcasted_iota(jnp.int32, s.shape, 1)
            s = jnp.where(col <= row, s, NEG_INF)
        chunks = [s[:, c * LANES:(c + 1) * LANES] for c in range(nc)]
        bm = jnp.max(functools.reduce(jnp.maximum, chunks), axis=-1, keepdims=True)
        if first:
            m_new = jnp.broadcast_to(bm, (tq, LANES))
        else:
            m_prev = m_sc[qi]
            m_new = jnp.maximum(m_prev, bm)
            alpha = jnp.exp2(m_prev - m_new)
        p = jnp.concatenate([jnp.exp2(c - m_new).astype(BF16) for c in chunks], axis=1)
        pv = _dot(p, vaug_sc[kb * tk:(kb + 1) * tk, :])
        acc = pv if first else jnp.concatenate([alpha, alpha], axis=1) * acc_sc[qi] + pv
        if last:
            o_ref[qi * tq:(qi + 1) * tq, :] = acc[:, :HEAD_DIM] / acc[:, HEAD_DIM:HEAD_DIM + 1]
        else:
            acc_sc[qi] = acc
            m_sc[qi] = m_new

    scores(0)
    for n in range(len(blocks)):
        if n + 1 < len(blocks):
            scores(n + 1)
        softmax_pv(n)


def _fox_prompt(q, k, v, ck, b, t, h):
    tq = _pick(t, (512, 256, 128))
    nq = t // tq
    blk = pl.BlockSpec((t, HEAD_DIM), lambda i, j: (i, j))
    return pl.pallas_call(
        functools.partial(_fox_prompt_kernel, tq=tq, tk=tq),
        grid=(b, h),
        in_specs=[blk, blk, blk, pl.BlockSpec((None, None, 1, t), lambda i, j: (i, j, 0, 0))],
        out_specs=blk,
        out_shape=jax.ShapeDtypeStruct((b * t, h * HEAD_DIM), F32),
        scratch_shapes=[pltpu.VMEM((t, 2 * HEAD_DIM), BF16), pltpu.VMEM((t, 2 * HEAD_DIM), BF16),
                        pltpu.VMEM((nq, tq, LANES), F32), pltpu.VMEM((nq, tq, 2 * HEAD_DIM), F32),
                        pltpu.VMEM((SCORE_BUFFERS, tq, tq), F32)],
        compiler_params=_params("parallel", "parallel"),
        name="fox_prompt",
    )(q, k, v, ck)


def _tri2(tk):
    idx = jnp.arange(tk)
    tri = (idx[:, None] >= idx[None, :]).astype(BF16)
    return jnp.concatenate([tri, tri], axis=0)


def _sb_prompt_kernel(q_ref, k_ref, v_ref, tri2_ref, o_ref, k_sc, v_sc, carry_sc, acc_sc, z_sc, *, tq, tk):
    t = k_ref.shape[0]
    nq = t // tq
    nbuf = z_sc.shape[0]
    k_sc[...] = k_ref[...].astype(BF16)
    v_sc[...] = v_ref[...].astype(BF16)
    tri2 = tri2_ref[...]
    nc = tk // LANES
    blocks = [(qi, kb) for qi in range(nq) for kb in reversed(range(((qi + 1) * tq) // tk))]

    def scores(n):
        qi, kb = blocks[n]
        z_sc[n % nbuf] = _dot_nt(q_ref[qi * tq:(qi + 1) * tq, :], k_sc[kb * tk:(kb + 1) * tk, :])

    def weights_pv(n):
        qi, kb = blocks[n]
        first = kb == ((qi + 1) * tq) // tk - 1
        final = kb == 0
        diagonal = (kb + 1) * tk > qi * tq
        z = z_sc[n % nbuf]
        sp = _softplus2(z)
        if diagonal:
            row = qi * tq + lax.broadcasted_iota(jnp.int32, z.shape, 0)
            col = kb * tk + lax.broadcasted_iota(jnp.int32, z.shape, 1)
            valid = col < row
            sp = jnp.where(valid, sp, 0.0)
        incl = _dot(_split2(sp), tri2)
        arg = z - incl
        if not first:
            carry = carry_sc[qi]
            arg = arg - jnp.concatenate([carry] * nc, axis=1)
        a = jnp.exp2(arg)
        if diagonal:
            a = jnp.where(valid, a, 0.0)
        pv = _dot(a.astype(BF16), v_sc[kb * tk:(kb + 1) * tk, :])
        acc = pv if first else acc_sc[qi] + pv
        if final:
            o_ref[qi * tq:(qi + 1) * tq, :] = acc
        else:
            acc_sc[qi] = acc
            tot = incl[:, :1]
            carry_sc[qi] = jnp.broadcast_to(tot, (tq, LANES)) if first else carry + tot

    scores(0)
    for n in range(len(blocks)):
        if n + 1 < len(blocks):
            scores(n + 1)
        weights_pv(n)


def _sb_prompt(q, k, v, b, t, h):
    tq = _pick(t, (512, 256, 128))
    tk = _pick(tq, (256, 128))
    nq = t // tq
    blk = pl.BlockSpec((t, HEAD_DIM), lambda i, j: (i, j))
    return pl.pallas_call(
        functools.partial(_sb_prompt_kernel, tq=tq, tk=tk),
        grid=(b, h),
        in_specs=[blk, blk, blk, pl.BlockSpec((2 * tk, tk), lambda i, j: (0, 0))],
        out_specs=blk,
        out_shape=jax.ShapeDtypeStruct((b * t, h * HEAD_DIM), F32),
        scratch_shapes=[pltpu.VMEM((t, HEAD_DIM), BF16), pltpu.VMEM((t, HEAD_DIM), BF16),
                        pltpu.VMEM((nq, tq, LANES), F32), pltpu.VMEM((nq, tq, HEAD_DIM), F32),
                        pltpu.VMEM((SCORE_BUFFERS, tq, tk), F32)],
        compiler_params=_params("parallel", "parallel"),
        name="sb_prompt",
    )(q, k, v, _tri2(tk))


def _head_rows(ref, h, nh, past):
    return ref[pl.ds(h, past, stride=nh), :].astype(BF16)


def _fox_sample_kernel(q_ref, kc_ref, vc_ref, kn_ref, vn_ref, ckc_ref, ckn_ref, o_ref, *, nh, past):
    ts = q_ref.shape[0]
    row = lax.broadcasted_iota(jnp.int32, (ts, ts), 0)
    col = lax.broadcasted_iota(jnp.int32, (ts, ts), 1)
    for h in range(nh):
        sl = slice(h * HEAD_DIM, (h + 1) * HEAD_DIM)
        q = q_ref[:, sl]
        s1 = _dot_nt(q, _head_rows(kc_ref, h, nh, past)) - ckc_ref[h:h + 1, :] * LOG2E
        s2 = _dot_nt(q, kn_ref[:, sl].astype(BF16)) - ckn_ref[h:h + 1, :] * LOG2E
        s2 = jnp.where(col <= row, s2, NEG_INF)
        m = jnp.maximum(jnp.max(s1, axis=-1, keepdims=True), jnp.max(s2, axis=-1, keepdims=True))
        p1 = jnp.exp2(s1 - m)
        p2 = jnp.exp2(s2 - m)
        l = jnp.sum(p1, axis=-1, keepdims=True) + jnp.sum(p2, axis=-1, keepdims=True)
        acc = _dot(p1.astype(BF16), _head_rows(vc_ref, h, nh, past)) + _dot(p2.astype(BF16),
                                                                             vn_ref[:, sl].astype(BF16))
        o_ref[:, sl] = acc / l


def _sb_sample_kernel(q_ref, kc_ref, vc_ref, kn_ref, vn_ref, tric_ref, trin_ref, o_ref, *, nh, past, tc):
    ts = q_ref.shape[0]
    row = lax.broadcasted_iota(jnp.int32, (ts, ts), 0)
    col = lax.broadcasted_iota(jnp.int32, (ts, ts), 1)
    valid = col < row
    tric = tric_ref[...]
    trin = trin_ref[...]
    nchunk = past // tc
    for h in range(nh):
        sl = slice(h * HEAD_DIM, (h + 1) * HEAD_DIM)
        q = q_ref[:, sl]
        vc = _head_rows(vc_ref, h, nh, past)
        z2 = _dot_nt(q, kn_ref[:, sl].astype(BF16))
        incl2 = _dot(_split2(jnp.where(valid, _softplus2(z2), 0.0)), trin)
        a2 = jnp.where(valid, jnp.exp2(z2 - incl2), 0.0)
        acc = _dot(a2.astype(BF16), vn_ref[:, sl].astype(BF16))
        carry = incl2[:, :1]
        z1 = _dot_nt(q, _head_rows(kc_ref, h, nh, past))
        sp1 = _softplus2(z1)
        stacked = jnp.concatenate([sp1[:, c * tc:(c + 1) * tc] for c in range(nchunk)], axis=0)
        incl = _dot(_split2(stacked), tric)
        for c in reversed(range(nchunk)):
            inc_c = incl[c * ts:(c + 1) * ts, :]
            a = jnp.exp2(z1[:, c * tc:(c + 1) * tc] - inc_c - carry)
            acc = acc + _dot(a.astype(BF16), vc[c * tc:(c + 1) * tc, :])
            carry = carry + inc_c[:, :1]
        o_ref[:, sl] = acc


def _sample_specs(ts, past, nh):
    new = pl.BlockSpec((ts, nh * HEAD_DIM), lambda i: (i, 0))
    cache = pl.BlockSpec((None, past * nh, HEAD_DIM), lambda i: (i, 0, 0))
    return new, cache


def _fox_sample(q, kc, vc, kn, vn, ckc, ckn, b, ts, past, h):
    new, cache = _sample_specs(ts, past, h)
    return pl.pallas_call(
        functools.partial(_fox_sample_kernel, nh=h, past=past),
        grid=(b,),
        in_specs=[new, cache, cache, new, new,
                  pl.BlockSpec((None, h, past), lambda i: (i, 0, 0)),
                  pl.BlockSpec((None, h, ts), lambda i: (i, 0, 0))],
        out_specs=new,
        out_shape=jax.ShapeDtypeStruct((b * ts, h * HEAD_DIM), F32),
        compiler_params=_params("parallel"),
        name="fox_sample",
    )(q, kc, vc, kn, vn, ckc, ckn)


def _sb_sample(q, kc, vc, kn, vn, b, ts, past, h):
    new, cache = _sample_specs(ts, past, h)
    tc = _pick(past, (256, 128))
    return pl.pallas_call(
        functools.partial(_sb_sample_kernel, nh=h, past=past, tc=tc),
        grid=(b,),
        in_specs=[new, cache, cache, new, new,
                  pl.BlockSpec((2 * tc, tc), lambda i: (0, 0)),
                  pl.BlockSpec((2 * ts, ts), lambda i: (0, 0))],
        out_specs=new,
        out_shape=jax.ShapeDtypeStruct((b * ts, h * HEAD_DIM), F32),
        compiler_params=_params("parallel"),
        name="sb_sample",
    )(q, kc, vc, kn, vn, _tri2(tc), _tri2(ts))


def _layer(x, b, t, w, caches):
    hf = w["b_forget"].shape[0]
    hs = w["w_sb"][0].shape[1] // HEAD_DIM
    q_scale = HEAD_DIM ** -0.5 * LOG2E
    big = (1024, 512, 256, 128)

    hn = _rmsnorm(x, w["attn_norm"], BF16)
    qkv = {}
    for grp in ("fox", "sb"):
        wq, wk, wv = w["w_" + grp]
        (q,) = _matmul_call(functools.partial(_proj_q_kernel, scale=q_scale), hn, [wq], [], [], [BF16], big,
                            "proj_q_" + grp)
        (k,) = _matmul_call(_proj_f32_kernel, hn, [wk], [], [], [F32], big, "proj_k_" + grp)
        (v,) = _matmul_call(_proj_f32_kernel, hn, [wv], [], [], [F32], big, "proj_v_" + grp)
        qkv[grp] = (q, k, v)
    (logf_pad,) = _matmul_call(_proj_logf_kernel, hn, [w["w_fl"]], [w["b_fl"]], [], [F32], (LANES,), "proj_logf")

    qf, kf, vf = qkv["fox"]
    qs, ks, vs = qkv["sb"]
    zeros_init = jnp.zeros((b, 1, LANES), F32)
    if caches is None:
        ct, _ = _cumsum_t(logf_pad.reshape(b, t, LANES), zeros_init)
        o_fox = _fox_prompt(qf, kf, vf, ct[:, :hf, :].reshape(b, hf, 1, t), b, t, hf)
        o_sb = _sb_prompt(qs, ks, vs, b, t, hs)
    else:
        c_fk, c_fv, c_fl, c_sk, c_sv = caches
        past = c_fk.shape[1]
        c_fl_pad = jnp.pad(c_fl.astype(F32), ((0, 0), (0, 0), (0, LANES - hf)))
        ct_c, last = _cumsum_t(c_fl_pad, zeros_init)
        ct_n, _ = _cumsum_t(logf_pad.reshape(b, t, LANES), last)
        o_fox = _fox_sample(qf, c_fk.reshape(b, past * hf, HEAD_DIM), c_fv.reshape(b, past * hf, HEAD_DIM), kf, vf,
                            ct_c[:, :hf, :], ct_n[:, :hf, :], b, t, past, hf)
        o_sb = _sb_sample(qs, c_sk.reshape(b, past * hs, HEAD_DIM), c_sv.reshape(b, past * hs, HEAD_DIM), ks, vs,
                          b, t, past, hs)

    a = _merge_norm(o_fox, o_sb, w["out_norm_fox"], w["out_norm_sb"])
    (x1,) = _matmul_call(_proj_resid_kernel, a, [w["w_out"]], [], [x], [F32], big, "proj_out")
    h2 = _rmsnorm(x1, w["ffn_norm"], BF16)
    (act,) = _matmul_call(_proj_swiglu_kernel, h2, [w["w_gate"], w["w_up"]], [], [], [BF16], (256, 128), "ffn_up")
    (x2,) = _matmul_call(_proj_resid_kernel, act, [w["w_down"]], [], [x1], [F32], (256, 128), "ffn_down",
                         tm_candidates=(512, 256, 128, 64, 32, 16, 8))
    return x2, kf, vf, logf_pad[:, :hf], ks, vs


def kernel(x_prompt, x_sample, cache_fox_k, cache_fox_v, cache_fox_logf, cache_sb_k, cache_sb_v, attn_norm, w_in,
           b_forget, out_norm_fox, out_norm_sb, w_out, ffn_norm, w_gate, w_up, w_down, final_norm):
    bp, tp, d = x_prompt.shape
    bs, ts, _ = x_sample.shape
    depth = w_in.shape[0]
    hf = cache_fox_k.shape[3]
    hs = cache_sb_k.shape[3]
    w_fox = hf * HEAD_DIM
    w_sb = hs * HEAD_DIM

    xp = x_prompt.reshape(bp * tp, d)
    xs = x_sample.reshape(bs * ts, d)
    outs_p, outs_s = [], []
    for l in range(depth):
        wl = w_in[l]
        cuts = [0, w_fox, 2 * w_fox, 3 * w_fox, 3 * w_fox + hf, 3 * w_fox + hf + w_sb, 3 * w_fox + hf + 2 * w_sb,
                3 * w_fox + hf + 3 * w_sb]
        seg = [wl[:, cuts[i]:cuts[i + 1]] for i in range(7)]
        w = {
            "attn_norm": attn_norm[l], "ffn_norm": ffn_norm[l],
            "out_norm_fox": out_norm_fox[l], "out_norm_sb": out_norm_sb[l],
            "w_fox": [s.astype(BF16) for s in seg[0:3]],
            "w_sb": [s.astype(BF16) for s in seg[4:7]],
            "w_fl": jnp.pad(seg[3], ((0, 0), (0, LANES - hf))).astype(BF16),
            "b_fl": jnp.pad(b_forget[l].astype(F32), (0, LANES - hf)).reshape(1, LANES),
            "b_forget": b_forget[l],
            "w_out": w_out[l].astype(BF16), "w_gate": w_gate[l].astype(BF16), "w_up": w_up[l].astype(BF16),
            "w_down": w_down[l].astype(BF16),
        }
        xp, *rest_p = _layer(xp, bp, tp, w, None)
        caches = (cache_fox_k[l], cache_fox_v[l], cache_fox_logf[l], cache_sb_k[l], cache_sb_v[l])
        xs, *rest_s = _layer(xs, bs, ts, w, caches)
        outs_p.append(rest_p)
        outs_s.append(rest_s)

    y_prompt = _rmsnorm(xp, final_norm, F32).reshape(bp, tp, d)
    y_sample = _rmsnorm(xs, final_norm, F32).reshape(bs, ts, d)

    def stack(outs, idx, shape):
        return jnp.stack([o[idx].reshape(shape) for o in outs])

    res = [y_prompt, y_sample]
    for outs, b, t in ((outs_p, bp, tp), (outs_s, bs, ts)):
        res += [stack(outs, 0, (b, t, hf, HEAD_DIM)), stack(outs, 1, (b, t, hf, HEAD_DIM)),
                stack(outs, 2, (b, t, hf)),
                stack(outs, 3, (b, t, hs, HEAD_DIM)), stack(outs, 4, (b, t, hs, HEAD_DIM))]
    return tuple(res)
```

```python
import functools
import math

import jax
import jax.numpy as jnp
from jax import lax
from jax.experimental import pallas as pl
from jax.experimental.pallas import tpu as pltpu

EPS = 1e-6
NEG_INF = -1e30
HEAD_DIM = 128
LANES = 128
LOG2E = math.log2(math.e)
VMEM_LIMIT_BYTES = 56 * 1024 * 1024
SCORE_BUFFERS = 3

F32 = jnp.float32
BF16 = jnp.bfloat16


def _params(*sem):
    return pltpu.CompilerParams(dimension_semantics=sem, vmem_limit_bytes=VMEM_LIMIT_BYTES)


def _pick(n, candidates):
    for c in candidates:
        if n % c == 0:
            return c
    return n


def _softplus(z):
    return jnp.maximum(z, 0.0) + jnp.log1p(jnp.exp(-jnp.abs(z)))


def _softplus2(z):
    return jnp.maximum(z, 0.0) + jnp.log2(1.0 + jnp.exp2(-jnp.abs(z)))


def _dot(a, b):
    return jnp.dot(a, b, preferred_element_type=F32)


def _dot_nt(a, b):
    return lax.dot_general(a, b, (((1,), (1,)), ((), ())), preferred_element_type=F32)


def _split2(x):
    hi = x.astype(BF16)
    lo = (x - hi.astype(F32)).astype(BF16)
    return jnp.concatenate([hi, lo], axis=1)


def _split3(x):
    hi = x.astype(BF16)
    r = x - hi.astype(F32)
    mid = r.astype(BF16)
    lo = (r - mid.astype(F32)).astype(BF16)
    return hi, mid, lo


def _rmsnorm_kernel(x_ref, g_ref, o_ref):
    x = x_ref[...]
    r = lax.rsqrt(jnp.mean(x * x, axis=-1, keepdims=True) + EPS)
    o_ref[...] = (x * r * g_ref[...]).astype(o_ref.dtype)


def _rmsnorm(x, g, out_dtype):
    m, d = x.shape
    tm = _pick(m, (256, 128, 64, 32, 16, 8))
    return pl.pallas_call(
        _rmsnorm_kernel,
        grid=(m // tm,),
        in_specs=[pl.BlockSpec((tm, d), lambda i: (i, 0)),
                  pl.BlockSpec((1, d), lambda i: (0, 0))],
        out_specs=pl.BlockSpec((tm, d), lambda i: (i, 0)),
        out_shape=jax.ShapeDtypeStruct((m, d), out_dtype),
        compiler_params=_params("parallel"),
        name="rmsnorm",
    )(x, g.reshape(1, d).astype(F32))


def _merge_norm_kernel(a_ref, b_ref, ga_ref, gb_ref, o_ref):
    wa = a_ref.shape[1]
    for ref, g_ref, lo in ((a_ref, ga_ref, 0), (b_ref, gb_ref, wa)):
        x = ref[...]
        r = lax.rsqrt(jnp.mean(x * x, axis=-1, keepdims=True) + EPS)
        o_ref[:, lo:lo + x.shape[1]] = (x * r * g_ref[...]).astype(o_ref.dtype)


def _merge_norm(o_fox, o_sb, g_fox, g_sb):
    m, wa = o_fox.shape
    wb = o_sb.shape[1]
    tm = _pick(m, (256, 128, 64, 32, 16, 8))
    return pl.pallas_call(
        _merge_norm_kernel,
        grid=(m // tm,),
        in_specs=[pl.BlockSpec((tm, wa), lambda i: (i, 0)),
                  pl.BlockSpec((tm, wb), lambda i: (i, 0)),
                  pl.BlockSpec((1, wa), lambda i: (0, 0)),
                  pl.BlockSpec((1, wb), lambda i: (0, 0))],
        out_specs=pl.BlockSpec((tm, wa + wb), lambda i: (i, 0)),
        out_shape=jax.ShapeDtypeStruct((m, wa + wb), BF16),
        compiler_params=_params("parallel"),
        name="merge_norm",
    )(o_fox, o_sb, g_fox.reshape(1, wa).astype(F32), g_sb.reshape(1, wb).astype(F32))


def _proj_q_kernel(x_ref, w_ref, o_ref, *, scale):
    o_ref[...] = (_dot(x_ref[...], w_ref[...]) * scale).astype(o_ref.dtype)


def _proj_kv_kernel(x_ref, w_ref, o32_ref, o16_ref, *, nh):
    acc = _dot(x_ref[...], w_ref[...])
    o16_ref[...] = acc.astype(o16_ref.dtype)
    tm = x_ref.shape[0]
    for h in range(nh):
        o32_ref[pl.ds(h, tm, stride=nh), :] = acc[:, h * HEAD_DIM:(h + 1) * HEAD_DIM]


def _proj_kv(x, w, name):
    m, k = x.shape
    n = w.shape[1]
    nh = n // HEAD_DIM
    tm = _pick(m, (512, 256, 128, 64, 32, 16, 8))
    return pl.pallas_call(
        functools.partial(_proj_kv_kernel, nh=nh),
        grid=(m // tm,),
        in_specs=[pl.BlockSpec((tm, k), lambda i: (i, 0)),
                  pl.BlockSpec((k, n), lambda i: (0, 0))],
        out_specs=[pl.BlockSpec((tm * nh, HEAD_DIM), lambda i: (i, 0)),
                   pl.BlockSpec((tm, n), lambda i: (i, 0))],
        out_shape=[jax.ShapeDtypeStruct((m * nh, HEAD_DIM), F32), jax.ShapeDtypeStruct((m, n), BF16)],
        compiler_params=_params("parallel"),
        name=name,
    )(x, w)


def _proj_logf_kernel(x_ref, w_ref, b_ref, o_ref):
    o_ref[...] = -_softplus(-(_dot(x_ref[...], w_ref[...]) + b_ref[...]))


def _proj_resid_kernel(x_ref, w_ref, r_ref, o_ref):
    o_ref[...] = r_ref[...] + _dot(x_ref[...], w_ref[...])


def _proj_swiglu_kernel(x_ref, wg_ref, wu_ref, o_ref):
    x = x_ref[...]
    g = _dot(x, wg_ref[...])
    u = _dot(x, wu_ref[...])
    o_ref[...] = (g * jax.nn.sigmoid(g) * u).astype(o_ref.dtype)


def _matmul_call(kernel, x, weights, extra_row_inputs, extra_tile_inputs, out_dtypes, tn_candidates, name,
                 tm_candidates=(1024, 512, 256, 128, 64, 32, 16, 8)):
    m, k = x.shape
    n = weights[0].shape[1]
    tm = _pick(m, tm_candidates)
    tn = _pick(n, tn_candidates)
    tile = pl.BlockSpec((tm, tn), lambda i, j: (i, j))
    in_specs = [pl.BlockSpec((tm, k), lambda i, j: (i, 0))]
    in_specs += [pl.BlockSpec((k, tn), lambda i, j: (0, j)) for _ in weights]
    in_specs += [pl.BlockSpec((1, tn), lambda i, j: (0, j)) for _ in extra_row_inputs]
    in_specs += [tile for _ in extra_tile_inputs]
    outs = pl.pallas_call(
        kernel,
        grid=(m // tm, n // tn),
        in_specs=in_specs,
        out_specs=[tile for _ in out_dtypes],
        out_shape=[jax.ShapeDtypeStruct((m, n), dt) for dt in out_dtypes],
        compiler_params=_params("parallel", "arbitrary"),
        name=name,
    )(x, *weights, *extra_row_inputs, *extra_tile_inputs)
    return outs


def _cumsum_kernel(x_ref, init_ref, tril_ref, ct_ref, last_ref, carry_sc):
    t = pl.program_id(1)

    @pl.when(t == 0)
    def _():
        carry_sc[...] = init_ref[...]

    parts = jnp.concatenate(_split3(x_ref[...]), axis=1)
    s = _dot(tril_ref[...], parts)
    cum = s[:, :LANES] + s[:, LANES:2 * LANES] + s[:, 2 * LANES:] + carry_sc[...]
    carry_sc[...] = cum[-1:, :]
    ct_ref[...] = cum.T
    last_ref[...] = cum[-1:, :]


def _cumsum_t(x, init):
    b, t, _ = x.shape
    tt = _pick(t, (512, 256, 128))
    idx = jnp.arange(tt)
    tril = (idx[:, None] >= idx[None, :]).astype(BF16)
    return pl.pallas_call(
        _cumsum_kernel,
        grid=(b, t // tt),
        in_specs=[pl.BlockSpec((None, tt, LANES), lambda i, j: (i, j, 0)),
                  pl.BlockSpec((None, 1, LANES), lambda i, j: (i, 0, 0)),
                  pl.BlockSpec((tt, tt), lambda i, j: (0, 0))],
        out_specs=[pl.BlockSpec((None, LANES, tt), lambda i, j: (i, 0, j)),
                   pl.BlockSpec((None, 1, LANES), lambda i, j: (i, 0, 0))],
        out_shape=[jax.ShapeDtypeStruct((b, LANES, t), F32),
                   jax.ShapeDtypeStruct((b, 1, LANES), F32)],
        scratch_shapes=[pltpu.VMEM((1, LANES), F32)],
        compiler_params=_params("parallel", "arbitrary"),
        name="cumsum_t",
    )(x, init, tril)


def _fox_prompt_kernel(q_ref, k_ref, v_ref, ck_ref, o_ref, kaug_sc, vaug_sc, m_sc, acc_sc, s_sc, *, tq, tk):
    t = k_ref.shape[0]
    nq = t // tq
    nbuf = s_sc.shape[0]
    hi, mid, lo = _split3(ck_ref[...] * LOG2E)
    rows = lax.broadcasted_iota(jnp.int32, (HEAD_DIM, t), 0)
    parts = jnp.where(rows == 0, hi.astype(F32),
                      jnp.where(rows == 1, mid.astype(F32), jnp.where(rows == 2, lo.astype(F32), 0.0)))
    kaug_sc[:, :HEAD_DIM] = k_ref[...]
    kaug_sc[:, HEAD_DIM:] = parts.T.astype(BF16)
    lane = lax.broadcasted_iota(jnp.int32, (t, HEAD_DIM), 1)
    vaug_sc[:, :HEAD_DIM] = v_ref[...]
    vaug_sc[:, HEAD_DIM:] = jnp.where(lane == 0, 1.0, 0.0).astype(BF16)

    lane_q = lax.broadcasted_iota(jnp.int32, (tq, HEAD_DIM), 1)
    q_extra = jnp.where(lane_q < 3, -1.0, 0.0).astype(BF16)
    nc = tk // LANES
    blocks = [(qi, kb) for qi in range(nq) for kb in range((qi * tq) // tk + 1)]

    def scores(n):
        qi, kb = blocks[n]
        qa = jnp.concatenate([q_ref[qi * tq:(qi + 1) * tq, :], q_extra], axis=1)
        s_sc[n % nbuf] = _dot_nt(qa, kaug_sc[kb * tk:(kb + 1) * tk, :])

    def softmax_pv(n):
        qi, kb = blocks[n]
        first = kb == 0
        last = kb == (qi * tq) // tk
        s = s_sc[n % nbuf]
        if last:
            row = qi * tq + lax.broadcasted_iota(jnp.int32, s.shape, 0)
            col = kb * tk + lax.broadcasted_iota(jnp.int32, s.shape, 1)
            s = jnp.where(col <= row, s, NEG_INF)
        chunks = [s[:, c * LANES:(c + 1) * LANES] for c in range(nc)]
        bm = jnp.max(functools.reduce(jnp.maximum, chunks), axis=-1, keepdims=True)
        if first:
            m_new = jnp.broadcast_to(bm, (tq, LANES))
        else:
            m_prev = m_sc[qi]
            m_new = jnp.maximum(m_prev, bm)
            alpha = jnp.exp2(m_prev - m_new)
        p = jnp.concatenate([jnp.exp2(c - m_new).astype(BF16) for c in chunks], axis=1)
        pv = _dot(p, vaug_sc[kb * tk:(kb + 1) * tk, :])
        acc = pv if first else jnp.concatenate([alpha, alpha], axis=1) * acc_sc[qi] + pv
        if last:
            o_ref[qi * tq:(qi + 1) * tq, :] = acc[:, :HEAD_DIM] / acc[:, HEAD_DIM:HEAD_DIM + 1]
        else:
            acc_sc[qi] = acc
            m_sc[qi] = m_new

    scores(0)
    for n in range(len(blocks)):
        if n + 1 < len(blocks):
            scores(n + 1)
        softmax_pv(n)


def _fox_prompt(q, k, v, ck, b, t, h):
    tq = _pick(t, (512, 256, 128))
    nq = t // tq
    blk = pl.BlockSpec((t, HEAD_DIM), lambda i, j: (i, j))
    return pl.pallas_call(
        functools.partial(_fox_prompt_kernel, tq=tq, tk=tq),
        grid=(b, h),
        in_specs=[blk, blk, blk, pl.BlockSpec((None, None, 1, t), lambda i, j: (i, j, 0, 0))],
        out_specs=blk,
        out_shape=jax.ShapeDtypeStruct((b * t, h * HEAD_DIM), F32),
        scratch_shapes=[pltpu.VMEM((t, 2 * HEAD_DIM), BF16), pltpu.VMEM((t, 2 * HEAD_DIM), BF16),
                        pltpu.VMEM((nq, tq, LANES), F32), pltpu.VMEM((nq, tq, 2 * HEAD_DIM), F32),
                        pltpu.VMEM((SCORE_BUFFERS, tq, tq), F32)],
        compiler_params=_params("parallel", "parallel"),
        name="fox_prompt",
    )(q, k, v, ck)


def _tri2(tk):
    idx = jnp.arange(tk)
    tri = (idx[:, None] >= idx[None, :]).astype(BF16)
    return jnp.concatenate([tri, tri], axis=0)


SKIP_LOG2 = 152.0


def _sb_prompt_kernel(q_ref, k_ref, v_ref, tri2_ref, o_ref, carry_sc, acc_sc, z_sc, *, tb):
    t = k_ref.shape[0]
    nq = t // tb
    nbuf = z_sc.shape[0]
    tri2 = tri2_ref[...]
    nc = tb // LANES

    def q_tile(qi):
        return q_ref[qi * tb:(qi + 1) * tb, :]

    def weights(z, carry, diagonal):
        sp = _softplus2(z)
        if diagonal:
            row = lax.broadcasted_iota(jnp.int32, z.shape, 0)
            col = lax.broadcasted_iota(jnp.int32, z.shape, 1)
            valid = col < row
            sp = jnp.where(valid, sp, 0.0)
        incl = _dot(_split2(sp), tri2)
        arg = z - incl
        if carry is not None:
            arg = arg - jnp.concatenate([carry] * nc, axis=1)
        a = jnp.exp2(arg)
        if diagonal:
            a = jnp.where(valid, a, 0.0)
        tot = incl[:, :1]
        new_carry = jnp.broadcast_to(tot, (tb, LANES)) if carry is None else carry + tot
        return a.astype(BF16), new_carry

    blocks = [(qi, kb) for qi in range(nq) for kb in (qi, qi - 1) if kb >= 0]

    def scores(n):
        qi, kb = blocks[n]
        z_sc[n % nbuf] = _dot_nt(q_tile(qi), k_ref[kb * tb:(kb + 1) * tb, :])

    def near(n):
        qi, kb = blocks[n]
        first = kb == qi
        a, carry = weights(z_sc[n % nbuf], None if first else carry_sc[qi], first)
        pv = _dot(a, v_ref[kb * tb:(kb + 1) * tb, :])
        acc = pv if first else acc_sc[qi] + pv
        if kb == 0:
            o_ref[qi * tb:(qi + 1) * tb, :] = acc
        else:
            acc_sc[qi] = acc
            carry_sc[qi] = carry

    scores(0)
    for n in range(len(blocks)):
        if n + 1 < len(blocks):
            scores(n + 1)
        near(n)

    for qi in range(2, nq):
        def more(kb):
            return jnp.logical_and(kb >= 0, jnp.min(carry_sc[qi]) <= SKIP_LOG2)

        def far(kb):
            start = pl.multiple_of(kb * tb, tb)
            z = _dot_nt(q_tile(qi), k_ref[pl.ds(start, tb), :])
            a, carry = weights(z, carry_sc[qi], False)
            acc_sc[qi] = acc_sc[qi] + _dot(a, v_ref[pl.ds(start, tb), :])
            carry_sc[qi] = carry
            return kb - 1

        lax.while_loop(more, far, qi - 2)
        o_ref[qi * tb:(qi + 1) * tb, :] = acc_sc[qi]


def _sb_prompt(q, k, v, b, t, h):
    tb = _pick(t, (256, 128))
    nq = t // tb
    blk = pl.BlockSpec((t, HEAD_DIM), lambda i, j: (i, j))
    return pl.pallas_call(
        functools.partial(_sb_prompt_kernel, tb=tb),
        grid=(b, h),
        in_specs=[blk, blk, blk, pl.BlockSpec((2 * tb, tb), lambda i, j: (0, 0))],
        out_specs=blk,
        out_shape=jax.ShapeDtypeStruct((b * t, h * HEAD_DIM), F32),
        scratch_shapes=[pltpu.VMEM((nq, tb, LANES), F32), pltpu.VMEM((nq, tb, HEAD_DIM), F32),
                        pltpu.VMEM((SCORE_BUFFERS, tb, tb), F32)],
        compiler_params=_params("parallel", "parallel"),
        name="sb_prompt",
    )(q, k, v, _tri2(tb))


def _head_rows(ref, h, nh, past):
    return ref[pl.ds(h, past, stride=nh), :].astype(BF16)


def _fox_sample_kernel(q_ref, kc_ref, vc_ref, kn_ref, vn_ref, ckc_ref, ckn_ref, o_ref, *, nh, past):
    ts = q_ref.shape[0]
    row = lax.broadcasted_iota(jnp.int32, (ts, ts), 0)
    col = lax.broadcasted_iota(jnp.int32, (ts, ts), 1)
    for h in range(nh):
        sl = slice(h * HEAD_DIM, (h + 1) * HEAD_DIM)
        q = q_ref[:, sl]
        s1 = _dot_nt(q, _head_rows(kc_ref, h, nh, past)) - ckc_ref[h:h + 1, :] * LOG2E
        s2 = _dot_nt(q, kn_ref[:, sl]) - ckn_ref[h:h + 1, :] * LOG2E
        s2 = jnp.where(col <= row, s2, NEG_INF)
        m = jnp.maximum(jnp.max(s1, axis=-1, keepdims=True), jnp.max(s2, axis=-1, keepdims=True))
        p1 = jnp.exp2(s1 - m)
        p2 = jnp.exp2(s2 - m)
        l = jnp.sum(p1, axis=-1, keepdims=True) + jnp.sum(p2, axis=-1, keepdims=True)
        acc = _dot(p1.astype(BF16), _head_rows(vc_ref, h, nh, past)) + _dot(p2.astype(BF16), vn_ref[:, sl])
        o_ref[:, sl] = acc / l


def _sb_sample_kernel(q_ref, kc_ref, vc_ref, kn_ref, vn_ref, tric_ref, trin_ref, o_ref, *, nh, past, tc):
    ts = q_ref.shape[0]
    row = lax.broadcasted_iota(jnp.int32, (ts, ts), 0)
    col = lax.broadcasted_iota(jnp.int32, (ts, ts), 1)
    valid = col < row
    tric = tric_ref[...]
    trin = trin_ref[...]
    nchunk = past // tc
    for h in range(nh):
        sl = slice(h * HEAD_DIM, (h + 1) * HEAD_DIM)
        q = q_ref[:, sl]
        vc = _head_rows(vc_ref, h, nh, past)
        z2 = _dot_nt(q, kn_ref[:, sl])
        incl2 = _dot(_split2(jnp.where(valid, _softplus2(z2), 0.0)), trin)
        a2 = jnp.where(valid, jnp.exp2(z2 - incl2), 0.0)
        acc = _dot(a2.astype(BF16), vn_ref[:, sl])
        carry = incl2[:, :1]
        z1 = _dot_nt(q, _head_rows(kc_ref, h, nh, past))
        sp1 = _softplus2(z1)
        stacked = jnp.concatenate([sp1[:, c * tc:(c + 1) * tc] for c in range(nchunk)], axis=0)
        incl = _dot(_split2(stacked), tric)
        for c in reversed(range(nchunk)):
            inc_c = incl[c * ts:(c + 1) * ts, :]
            a = jnp.exp2(z1[:, c * tc:(c + 1) * tc] - inc_c - carry)
            acc = acc + _dot(a.astype(BF16), vc[c * tc:(c + 1) * tc, :])
            carry = carry + inc_c[:, :1]
        o_ref[:, sl] = acc


def _sample_specs(ts, past, nh):
    new = pl.BlockSpec((ts, nh * HEAD_DIM), lambda i: (i, 0))
    cache = pl.BlockSpec((None, past * nh, HEAD_DIM), lambda i: (i, 0, 0))
    return new, cache


def _fox_sample(q, kc, vc, kn, vn, ckc, ckn, b, ts, past, h):
    new, cache = _sample_specs(ts, past, h)
    return pl.pallas_call(
        functools.partial(_fox_sample_kernel, nh=h, past=past),
        grid=(b,),
        in_specs=[new, cache, cache, new, new,
                  pl.BlockSpec((None, h, past), lambda i: (i, 0, 0)),
                  pl.BlockSpec((None, h, ts), lambda i: (i, 0, 0))],
        out_specs=new,
        out_shape=jax.ShapeDtypeStruct((b * ts, h * HEAD_DIM), F32),
        compiler_params=_params("parallel"),
        name="fox_sample",
    )(q, kc, vc, kn, vn, ckc, ckn)


def _sb_sample(q, kc, vc, kn, vn, b, ts, past, h):
    new, cache = _sample_specs(ts, past, h)
    tc = _pick(past, (256, 128))
    return pl.pallas_call(
        functools.partial(_sb_sample_kernel, nh=h, past=past, tc=tc),
        grid=(b,),
        in_specs=[new, cache, cache, new, new,
                  pl.BlockSpec((2 * tc, tc), lambda i: (0, 0)),
                  pl.BlockSpec((2 * ts, ts), lambda i: (0, 0))],
        out_specs=new,
        out_shape=jax.ShapeDtypeStruct((b * ts, h * HEAD_DIM), F32),
        compiler_params=_params("parallel"),
        name="sb_sample",
    )(q, kc, vc, kn, vn, _tri2(tc), _tri2(ts))


def _layer(x, b, t, w, caches):
    hf = w["b_forget"].shape[0]
    hs = w["w_sb"][0].shape[1] // HEAD_DIM
    q_scale = HEAD_DIM ** -0.5 * LOG2E
    big = (1024, 512, 256, 128)

    hn = _rmsnorm(x, w["attn_norm"], BF16)
    qkv = {}
    for grp in ("fox", "sb"):
        wq, wk, wv = w["w_" + grp]
        (q,) = _matmul_call(functools.partial(_proj_q_kernel, scale=q_scale), hn, [wq], [], [], [BF16], big,
                            "proj_q_" + grp)
        k32, k16 = _proj_kv(hn, wk, "proj_k_" + grp)
        v32, v16 = _proj_kv(hn, wv, "proj_v_" + grp)
        qkv[grp] = (q, k32, k16, v32, v16)
    (logf_pad,) = _matmul_call(_proj_logf_kernel, hn, [w["w_fl"]], [w["b_fl"]], [], [F32], (LANES,), "proj_logf")

    qf, kf32, kf, vf32, vf = qkv["fox"]
    qs, ks32, ks, vs32, vs = qkv["sb"]
    zeros_init = jnp.zeros((b, 1, LANES), F32)
    if caches is None:
        ct, _ = _cumsum_t(logf_pad.reshape(b, t, LANES), zeros_init)
        o_fox = _fox_prompt(qf, kf, vf, ct[:, :hf, :].reshape(b, hf, 1, t), b, t, hf)
        o_sb = _sb_prompt(qs, ks, vs, b, t, hs)
    else:
        c_fk, c_fv, c_fl, c_sk, c_sv = caches
        past = c_fk.shape[1]
        c_fl_pad = jnp.pad(c_fl.astype(F32), ((0, 0), (0, 0), (0, LANES - hf)))
        ct_c, last = _cumsum_t(c_fl_pad, zeros_init)
        ct_n, _ = _cumsum_t(logf_pad.reshape(b, t, LANES), last)
        o_fox = _fox_sample(qf, c_fk.reshape(b, past * hf, HEAD_DIM), c_fv.reshape(b, past * hf, HEAD_DIM), kf, vf,
                            ct_c[:, :hf, :], ct_n[:, :hf, :], b, t, past, hf)
        o_sb = _sb_sample(qs, c_sk.reshape(b, past * hs, HEAD_DIM), c_sv.reshape(b, past * hs, HEAD_DIM), ks, vs,
                          b, t, past, hs)

    a = _merge_norm(o_fox, o_sb, w["out_norm_fox"], w["out_norm_sb"])
    (x1,) = _matmul_call(_proj_resid_kernel, a, [w["w_out"]], [], [x], [F32], big, "proj_out")
    h2 = _rmsnorm(x1, w["ffn_norm"], BF16)
    (act,) = _matmul_call(_proj_swiglu_kernel, h2, [w["w_gate"], w["w_up"]], [], [], [BF16], (256, 128), "ffn_up",
                          tm_candidates=(2048, 1024, 512, 256, 128, 64, 32, 16, 8))
    (x2,) = _matmul_call(_proj_resid_kernel, act, [w["w_down"]], [], [x1], [F32], (512, 256, 128), "ffn_down",
                         tm_candidates=(512, 256, 128, 64, 32, 16, 8))
    return x2, kf32, vf32, logf_pad[:, :hf], ks32, vs32


def kernel(x_prompt, x_sample, cache_fox_k, cache_fox_v, cache_fox_logf, cache_sb_k, cache_sb_v, attn_norm, w_in,
           b_forget, out_norm_fox, out_norm_sb, w_out, ffn_norm, w_gate, w_up, w_down, final_norm):
    bp, tp, d = x_prompt.shape
    bs, ts, _ = x_sample.shape
    depth = w_in.shape[0]
    hf = cache_fox_k.shape[3]
    hs = cache_sb_k.shape[3]
    w_fox = hf * HEAD_DIM
    w_sb = hs * HEAD_DIM

    xp = x_prompt.reshape(bp * tp, d)
    xs = x_sample.reshape(bs * ts, d)
    outs_p, outs_s = [], []
    for l in range(depth):
        wl = w_in[l]
        cuts = [0, w_fox, 2 * w_fox, 3 * w_fox, 3 * w_fox + hf, 3 * w_fox + hf + w_sb, 3 * w_fox + hf + 2 * w_sb,
                3 * w_fox + hf + 3 * w_sb]
        seg = [wl[:, cuts[i]:cuts[i + 1]] for i in range(7)]
        w = {
            "attn_norm": attn_norm[l], "ffn_norm": ffn_norm[l],
            "out_norm_fox": out_norm_fox[l], "out_norm_sb": out_norm_sb[l],
            "w_fox": [s.astype(BF16) for s in seg[0:3]],
            "w_sb": [s.astype(BF16) for s in seg[4:7]],
            "w_fl": jnp.pad(seg[3], ((0, 0), (0, LANES - hf))).astype(BF16),
            "b_fl": jnp.pad(b_forget[l].astype(F32), (0, LANES - hf)).reshape(1, LANES),
            "b_forget": b_forget[l],
            "w_out": w_out[l].astype(BF16), "w_gate": w_gate[l].astype(BF16), "w_up": w_up[l].astype(BF16),
            "w_down": w_down[l].astype(BF16),
        }
        xp, *rest_p = _layer(xp, bp, tp, w, None)
        caches = (cache_fox_k[l], cache_fox_v[l], cache_fox_logf[l], cache_sb_k[l], cache_sb_v[l])
        xs, *rest_s = _layer(xs, bs, ts, w, caches)
        outs_p.append(rest_p)
        outs_s.append(rest_s)

    y_prompt = _rmsnorm(xp, final_norm, F32).reshape(bp, tp, d)
    y_sample = _rmsnorm(xs, final_norm, F32).reshape(bs, ts, d)

    def stack(outs, idx, shape):
        return jnp.stack([o[idx].reshape(shape) for o in outs])

    res = [y_prompt, y_sample]
    for outs, b, t in ((outs_p, bp, tp), (outs_s, bs, ts)):
        res += [stack(outs, 0, (b, t, hf, HEAD_DIM)), stack(outs, 1, (b, t, hf, HEAD_DIM)),
                stack(outs, 2, (b, t, hf)),
                stack(outs, 3, (b, t, hs, HEAD_DIM)), stack(outs, 4, (b, t, hs, HEAD_DIM))]
    return tuple(res)
```

```python
import functools
import math

import jax
import jax.numpy as jnp
from jax import lax
from jax.experimental import pallas as pl
from jax.experimental.pallas import tpu as pltpu

EPS = 1e-6
NEG_INF = -1e30
HEAD_DIM = 128
LANES = 128
LOG2E = math.log2(math.e)
VMEM_LIMIT_BYTES = 56 * 1024 * 1024
SCORE_BUFFERS = 3

F32 = jnp.float32
BF16 = jnp.bfloat16


def _params(*sem):
    return pltpu.CompilerParams(dimension_semantics=sem, vmem_limit_bytes=VMEM_LIMIT_BYTES)


def _pick(n, candidates):
    for c in candidates:
        if n % c == 0:
            return c
    return n


def _softplus(z):
    return jnp.maximum(z, 0.0) + jnp.log1p(jnp.exp(-jnp.abs(z)))


def _softplus2(z):
    return jnp.maximum(z, 0.0) + jnp.log2(1.0 + jnp.exp2(-jnp.abs(z)))


def _dot(a, b):
    return jnp.dot(a, b, preferred_element_type=F32)


def _dot_nt(a, b):
    return lax.dot_general(a, b, (((1,), (1,)), ((), ())), preferred_element_type=F32)


def _split2(x):
    hi = x.astype(BF16)
    lo = (x - hi.astype(F32)).astype(BF16)
    return jnp.concatenate([hi, lo], axis=1)


def _split3(x):
    hi = x.astype(BF16)
    r = x - hi.astype(F32)
    mid = r.astype(BF16)
    lo = (r - mid.astype(F32)).astype(BF16)
    return hi, mid, lo


def _rmsnorm_kernel(x_ref, g_ref, o_ref):
    x = x_ref[...]
    r = lax.rsqrt(jnp.mean(x * x, axis=-1, keepdims=True) + EPS)
    o_ref[...] = (x * r * g_ref[...]).astype(o_ref.dtype)


def _rmsnorm(x, g, out_dtype):
    m, d = x.shape
    tm = _pick(m, (256, 128, 64, 32, 16, 8))
    return pl.pallas_call(
        _rmsnorm_kernel,
        grid=(m // tm,),
        in_specs=[pl.BlockSpec((tm, d), lambda i: (i, 0)),
                  pl.BlockSpec((1, d), lambda i: (0, 0))],
        out_specs=pl.BlockSpec((tm, d), lambda i: (i, 0)),
        out_shape=jax.ShapeDtypeStruct((m, d), out_dtype),
        compiler_params=_params("parallel"),
        name="rmsnorm",
    )(x, g.reshape(1, d).astype(F32))


def _proj_q_kernel(x_ref, w_ref, o_ref, *, scale):
    o_ref[...] = (_dot(x_ref[...], w_ref[...]) * scale).astype(o_ref.dtype)


def _proj_kv_kernel(x_ref, w_ref, o32_ref, o16_ref, *, nh):
    acc = _dot(x_ref[...], w_ref[...])
    o16_ref[...] = acc.astype(o16_ref.dtype)
    tm = x_ref.shape[0]
    for h in range(nh):
        o32_ref[pl.ds(h, tm, stride=nh), :] = acc[:, h * HEAD_DIM:(h + 1) * HEAD_DIM]


def _proj_kv(x, w, name):
    m, k = x.shape
    n = w.shape[1]
    nh = n // HEAD_DIM
    tm = _pick(m, (512, 256, 128, 64, 32, 16, 8))
    return pl.pallas_call(
        functools.partial(_proj_kv_kernel, nh=nh),
        grid=(m // tm,),
        in_specs=[pl.BlockSpec((tm, k), lambda i: (i, 0)),
                  pl.BlockSpec((k, n), lambda i: (0, 0))],
        out_specs=[pl.BlockSpec((tm * nh, HEAD_DIM), lambda i: (i, 0)),
                   pl.BlockSpec((tm, n), lambda i: (i, 0))],
        out_shape=[jax.ShapeDtypeStruct((m * nh, HEAD_DIM), F32), jax.ShapeDtypeStruct((m, n), BF16)],
        compiler_params=_params("parallel"),
        name=name,
    )(x, w)


def _proj_logf_kernel(x_ref, w_ref, b_ref, o_ref):
    o_ref[...] = -_softplus(-(_dot(x_ref[...], w_ref[...]) + b_ref[...]))


def _proj_resid_kernel(x_ref, w_ref, r_ref, o_ref):
    o_ref[...] = r_ref[...] + _dot(x_ref[...], w_ref[...])


def _proj_swiglu_kernel(x_ref, wg_ref, wu_ref, ssq_ref, o_ref):
    x = x_ref[...]
    r = lax.rsqrt(jnp.sum(ssq_ref[...], axis=-1, keepdims=True) / x.shape[1] + EPS)
    g = _dot(x, wg_ref[...]) * r
    u = _dot(x, wu_ref[...]) * r
    o_ref[...] = (g * jax.nn.sigmoid(g) * u).astype(o_ref.dtype)


def _proj_out_kernel(of_ref, os_ref, gf_ref, gs_ref, w_ref, res_ref, g_ref, x1_ref, h_ref, ssq_ref, a_sc):
    first = pl.program_id(1) == 0

    @pl.when(first)
    def _():
        lo = 0
        for ref, gain_ref in ((of_ref, gf_ref), (os_ref, gs_ref)):
            x = ref[...]
            r = lax.rsqrt(jnp.mean(x * x, axis=-1, keepdims=True) + EPS)
            a_sc[:, lo:lo + x.shape[1]] = (x * r * gain_ref[...]).astype(a_sc.dtype)
            lo += x.shape[1]

    tn = x1_ref.shape[1]
    x1 = res_ref[...] + _dot(a_sc[...], w_ref[...])
    x1_ref[...] = x1
    h_ref[...] = (x1 * g_ref[...]).astype(h_ref.dtype)
    sq = x1 * x1
    part = functools.reduce(jnp.add, [sq[:, c * LANES:(c + 1) * LANES] for c in range(tn // LANES)])

    @pl.when(first)
    def _():
        ssq_ref[...] = part

    @pl.when(jnp.logical_not(first))
    def _():
        ssq_ref[...] += part


def _proj_out(o_f, o_s, g_f, g_s, w, res, gain):
    m, n = res.shape
    wf, ws = o_f.shape[1], o_s.shape[1]
    tm = _pick(m, (512, 256, 128, 64, 32, 16, 8))
    tn = _pick(n, (1024, 512, 256, 128))
    tile = pl.BlockSpec((tm, tn), lambda i, j: (i, j))
    return pl.pallas_call(
        _proj_out_kernel,
        grid=(m // tm, n // tn),
        in_specs=[pl.BlockSpec((tm, wf), lambda i, j: (i, 0)), pl.BlockSpec((tm, ws), lambda i, j: (i, 0)),
                  pl.BlockSpec((1, wf), lambda i, j: (0, 0)), pl.BlockSpec((1, ws), lambda i, j: (0, 0)),
                  pl.BlockSpec((wf + ws, tn), lambda i, j: (0, j)), tile,
                  pl.BlockSpec((1, tn), lambda i, j: (0, j))],
        out_specs=[tile, tile, pl.BlockSpec((tm, LANES), lambda i, j: (i, 0))],
        out_shape=[jax.ShapeDtypeStruct((m, n), F32), jax.ShapeDtypeStruct((m, n), BF16),
                   jax.ShapeDtypeStruct((m, LANES), F32)],
        scratch_shapes=[pltpu.VMEM((tm, wf + ws), BF16)],
        compiler_params=_params("parallel", "arbitrary"),
        name="proj_out",
    )(o_f, o_s, g_f.reshape(1, wf).astype(F32), g_s.reshape(1, ws).astype(F32), w, res,
      gain.reshape(1, n).astype(F32))


def _matmul_call(kernel, x, weights, extra_row_inputs, extra_tile_inputs, out_dtypes, tn_candidates, name,
                 tm_candidates=(1024, 512, 256, 128, 64, 32, 16, 8), row_block_inputs=()):
    m, k = x.shape
    n = weights[0].shape[1]
    tm = _pick(m, tm_candidates)
    tn = _pick(n, tn_candidates)
    tile = pl.BlockSpec((tm, tn), lambda i, j: (i, j))
    in_specs = [pl.BlockSpec((tm, k), lambda i, j: (i, 0))]
    in_specs += [pl.BlockSpec((k, tn), lambda i, j: (0, j)) for _ in weights]
    in_specs += [pl.BlockSpec((1, tn), lambda i, j: (0, j)) for _ in extra_row_inputs]
    in_specs += [pl.BlockSpec((tm, a.shape[1]), lambda i, j: (i, 0)) for a in row_block_inputs]
    in_specs += [tile for _ in extra_tile_inputs]
    outs = pl.pallas_call(
        kernel,
        grid=(m // tm, n // tn),
        in_specs=in_specs,
        out_specs=[tile for _ in out_dtypes],
        out_shape=[jax.ShapeDtypeStruct((m, n), dt) for dt in out_dtypes],
        compiler_params=_params("parallel", "arbitrary"),
        name=name,
    )(x, *weights, *extra_row_inputs, *row_block_inputs, *extra_tile_inputs)
    return outs


def _split_cast_kernel(w_ref, *o_refs, cuts):
    for o_ref, lo, hi in zip(o_refs, cuts[:-1], cuts[1:]):
        x = w_ref[:, lo:hi].astype(o_ref.dtype)
        if hi - lo < o_ref.shape[1]:
            x = jnp.concatenate([x, jnp.zeros((x.shape[0], o_ref.shape[1] - (hi - lo)), o_ref.dtype)], axis=1)
        o_ref[...] = x


def _split_cast(w, l, cuts, widths):
    _, k, n = w.shape
    tk = _pick(k, (128, 64, 32, 16, 8))
    return pl.pallas_call(
        functools.partial(_split_cast_kernel, cuts=cuts),
        grid=(k // tk,),
        in_specs=[pl.BlockSpec((None, tk, n), lambda i: (l, i, 0))],
        out_specs=[pl.BlockSpec((tk, wd), lambda i: (i, 0)) for wd in widths],
        out_shape=[jax.ShapeDtypeStruct((k, wd), BF16) for wd in widths],
        compiler_params=_params("parallel"),
        name="split_cast",
    )(w)


def _cumsum_kernel(x_ref, init_ref, tril_ref, ct_ref, last_ref, carry_sc):
    t = pl.program_id(1)

    @pl.when(t == 0)
    def _():
        carry_sc[...] = init_ref[...]

    parts = jnp.concatenate(_split3(x_ref[...]), axis=1)
    s = _dot(tril_ref[...], parts)
    cum = s[:, :LANES] + s[:, LANES:2 * LANES] + s[:, 2 * LANES:] + carry_sc[...]
    carry_sc[...] = cum[-1:, :]
    ct_ref[...] = cum.T
    last_ref[...] = cum[-1:, :]


def _cumsum_t(x, init):
    b, t, _ = x.shape
    tt = _pick(t, (512, 256, 128))
    idx = jnp.arange(tt)
    tril = (idx[:, None] >= idx[None, :]).astype(BF16)
    return pl.pallas_call(
        _cumsum_kernel,
        grid=(b, t // tt),
        in_specs=[pl.BlockSpec((None, tt, LANES), lambda i, j: (i, j, 0)),
                  pl.BlockSpec((None, 1, LANES), lambda i, j: (i, 0, 0)),
                  pl.BlockSpec((tt, tt), lambda i, j: (0, 0))],
        out_specs=[pl.BlockSpec((None, LANES, tt), lambda i, j: (i, 0, j)),
                   pl.BlockSpec((None, 1, LANES), lambda i, j: (i, 0, 0))],
        out_shape=[jax.ShapeDtypeStruct((b, LANES, t), F32),
                   jax.ShapeDtypeStruct((b, 1, LANES), F32)],
        scratch_shapes=[pltpu.VMEM((1, LANES), F32)],
        compiler_params=_params("parallel", "arbitrary"),
        name="cumsum_t",
    )(x, init, tril)


def _fox_prompt_kernel(q_ref, k_ref, v_ref, ck_ref, o_ref, kaug_sc, vaug_sc, m_sc, acc_sc, s_sc, *, tq, tk):
    t = k_ref.shape[0]
    nq = t // tq
    nbuf = s_sc.shape[0]
    hi, mid, lo = _split3(ck_ref[...] * LOG2E)
    rows = lax.broadcasted_iota(jnp.int32, (HEAD_DIM, t), 0)
    parts = jnp.where(rows == 0, hi.astype(F32),
                      jnp.where(rows == 1, mid.astype(F32), jnp.where(rows == 2, lo.astype(F32), 0.0)))
    kaug_sc[:, :HEAD_DIM] = k_ref[...]
    kaug_sc[:, HEAD_DIM:] = parts.T.astype(BF16)
    lane = lax.broadcasted_iota(jnp.int32, (t, HEAD_DIM), 1)
    vaug_sc[:, :HEAD_DIM] = v_ref[...]
    vaug_sc[:, HEAD_DIM:] = jnp.where(lane == 0, 1.0, 0.0).astype(BF16)

    lane_q = lax.broadcasted_iota(jnp.int32, (tq, HEAD_DIM), 1)
    q_extra = jnp.where(lane_q < 3, -1.0, 0.0).astype(BF16)
    nc = tk // LANES
    blocks = [(qi, kb) for qi in range(nq) for kb in range((qi * tq) // tk + 1)]

    def scores(n):
        qi, kb = blocks[n]
        qa = jnp.concatenate([q_ref[qi * tq:(qi + 1) * tq, :], q_extra], axis=1)
        s_sc[n % nbuf] = _dot_nt(qa, kaug_sc[kb * tk:(kb + 1) * tk, :])

    def softmax_pv(n):
        qi, kb = blocks[n]
        first = kb == 0
        last = kb == (qi * tq) // tk
        s = s_sc[n % nbuf]
        if last:
            row = qi * tq + lax.broadcasted_iota(jnp.int32, s.shape, 0)
            col = kb * tk + lax.broadcasted_iota(jnp.int32, s.shape, 1)
            s = jnp.where(col <= row, s, NEG_INF)
        chunks = [s[:, c * LANES:(c + 1) * LANES] for c in range(nc)]
        bm = jnp.max(functools.reduce(jnp.maximum, chunks), axis=-1, keepdims=True)
        if first:
            m_new = jnp.broadcast_to(bm, (tq, LANES))
        else:
            m_prev = m_sc[qi]
            m_new = jnp.maximum(m_prev, bm)
            alpha = jnp.exp2(m_prev - m_new)
        p = jnp.concatenate([jnp.exp2(c - m_new).astype(BF16) for c in chunks], axis=1)
        pv = _dot(p, vaug_sc[kb * tk:(kb + 1) * tk, :])
        acc = pv if first else jnp.concatenate([alpha, alpha], axis=1) * acc_sc[qi] + pv
        if last:
            o_ref[qi * tq:(qi + 1) * tq, :] = acc[:, :HEAD_DIM] / acc[:, HEAD_DIM:HEAD_DIM + 1]
        else:
            acc_sc[qi] = acc
            m_sc[qi] = m_new

    scores(0)
    for n in range(len(blocks)):
        if n + 1 < len(blocks):
            scores(n + 1)
        softmax_pv(n)


def _fox_prompt(q, k, v, ck, b, t, h):
    tq = _pick(t, (512, 256, 128))
    nq = t // tq
    blk = pl.BlockSpec((t, HEAD_DIM), lambda i, j: (i, j))
    return pl.pallas_call(
        functools.partial(_fox_prompt_kernel, tq=tq, tk=tq),
        grid=(b, h),
        in_specs=[blk, blk, blk, pl.BlockSpec((None, None, 1, t), lambda i, j: (i, j, 0, 0))],
        out_specs=blk,
        out_shape=jax.ShapeDtypeStruct((b * t, h * HEAD_DIM), F32),
        scratch_shapes=[pltpu.VMEM((t, 2 * HEAD_DIM), BF16), pltpu.VMEM((t, 2 * HEAD_DIM), BF16),
                        pltpu.VMEM((nq, tq, LANES), F32), pltpu.VMEM((nq, tq, 2 * HEAD_DIM), F32),
                        pltpu.VMEM((SCORE_BUFFERS, tq, tq), F32)],
        compiler_params=_params("parallel", "parallel"),
        name="fox_prompt",
    )(q, k, v, ck)


def _tri2(tk):
    idx = jnp.arange(tk)
    tri = (idx[:, None] >= idx[None, :]).astype(BF16)
    return jnp.concatenate([tri, tri], axis=0)


SKIP_LOG2 = 152.0


def _sb_prompt_kernel(q_ref, k_ref, v_ref, tri2_ref, o_ref, carry_sc, acc_sc, z_sc, *, tb):
    t = k_ref.shape[0]
    nq = t // tb
    nbuf = z_sc.shape[0]
    tri2 = tri2_ref[...]
    nc = tb // LANES

    def q_tile(qi):
        return q_ref[qi * tb:(qi + 1) * tb, :]

    def weights(z, carry, diagonal):
        sp = _softplus2(z)
        if diagonal:
            row = lax.broadcasted_iota(jnp.int32, z.shape, 0)
            col = lax.broadcasted_iota(jnp.int32, z.shape, 1)
            valid = col < row
            sp = jnp.where(valid, sp, 0.0)
        incl = _dot(_split2(sp), tri2)
        arg = z - incl
        if carry is not None:
            arg = arg - jnp.concatenate([carry] * nc, axis=1)
        a = jnp.exp2(arg)
        if diagonal:
            a = jnp.where(valid, a, 0.0)
        tot = incl[:, :1]
        new_carry = jnp.broadcast_to(tot, (tb, LANES)) if carry is None else carry + tot
        return a.astype(BF16), new_carry

    blocks = [(qi, kb) for qi in range(nq) for kb in (qi, qi - 1) if kb >= 0]

    def scores(n):
        qi, kb = blocks[n]
        z_sc[n % nbuf] = _dot_nt(q_tile(qi), k_ref[kb * tb:(kb + 1) * tb, :])

    def near(n):
        qi, kb = blocks[n]
        first = kb == qi
        a, carry = weights(z_sc[n % nbuf], None if first else carry_sc[qi], first)
        pv = _dot(a, v_ref[kb * tb:(kb + 1) * tb, :])
        acc = pv if first else acc_sc[qi] + pv
        if kb == 0:
            o_ref[qi * tb:(qi + 1) * tb, :] = acc
        else:
            acc_sc[qi] = acc
            carry_sc[qi] = carry

    scores(0)
    for n in range(len(blocks)):
        if n + 1 < len(blocks):
            scores(n + 1)
        near(n)

    for qi in range(2, nq):
        def more(kb):
            return jnp.logical_and(kb >= 0, jnp.min(carry_sc[qi]) <= SKIP_LOG2)

        def far(kb):
            start = pl.multiple_of(kb * tb, tb)
            z = _dot_nt(q_tile(qi), k_ref[pl.ds(start, tb), :])
            a, carry = weights(z, carry_sc[qi], False)
            acc_sc[qi] = acc_sc[qi] + _dot(a, v_ref[pl.ds(start, tb), :])
            carry_sc[qi] = carry
            return kb - 1

        lax.while_loop(more, far, qi - 2)
        o_ref[qi * tb:(qi + 1) * tb, :] = acc_sc[qi]


def _sb_prompt(q, k, v, b, t, h):
    tb = _pick(t, (256, 128))
    nq = t // tb
    blk = pl.BlockSpec((t, HEAD_DIM), lambda i, j: (i, j))
    return pl.pallas_call(
        functools.partial(_sb_prompt_kernel, tb=tb),
        grid=(b, h),
        in_specs=[blk, blk, blk, pl.BlockSpec((2 * tb, tb), lambda i, j: (0, 0))],
        out_specs=blk,
        out_shape=jax.ShapeDtypeStruct((b * t, h * HEAD_DIM), F32),
        scratch_shapes=[pltpu.VMEM((nq, tb, LANES), F32), pltpu.VMEM((nq, tb, HEAD_DIM), F32),
                        pltpu.VMEM((SCORE_BUFFERS, tb, tb), F32)],
        compiler_params=_params("parallel", "parallel"),
        name="sb_prompt",
    )(q, k, v, _tri2(tb))


def _head_rows(ref, h, nh, past):
    return ref[pl.ds(h, past, stride=nh), :].astype(BF16)


def _fox_sample_kernel(q_ref, kc_ref, vc_ref, kn_ref, vn_ref, ckc_ref, ckn_ref, o_ref, *, nh, past):
    ts = q_ref.shape[0]
    row = lax.broadcasted_iota(jnp.int32, (ts, ts), 0)
    col = lax.broadcasted_iota(jnp.int32, (ts, ts), 1)
    for h in range(nh):
        sl = slice(h * HEAD_DIM, (h + 1) * HEAD_DIM)
        q = q_ref[:, sl]
        s1 = _dot_nt(q, _head_rows(kc_ref, h, nh, past)) - ckc_ref[h:h + 1, :] * LOG2E
        s2 = _dot_nt(q, kn_ref[:, sl]) - ckn_ref[h:h + 1, :] * LOG2E
        s2 = jnp.where(col <= row, s2, NEG_INF)
        m = jnp.maximum(jnp.max(s1, axis=-1, keepdims=True), jnp.max(s2, axis=-1, keepdims=True))
        p1 = jnp.exp2(s1 - m)
        p2 = jnp.exp2(s2 - m)
        l = jnp.sum(p1, axis=-1, keepdims=True) + jnp.sum(p2, axis=-1, keepdims=True)
        acc = _dot(p1.astype(BF16), _head_rows(vc_ref, h, nh, past)) + _dot(p2.astype(BF16), vn_ref[:, sl])
        o_ref[:, sl] = acc / l


def _sb_sample_kernel(q_ref, kc_ref, vc_ref, kn_ref, vn_ref, tric_ref, trin_ref, o_ref, *, nh, past, tc):
    ts = q_ref.shape[0]
    row = lax.broadcasted_iota(jnp.int32, (ts, ts), 0)
    col = lax.broadcasted_iota(jnp.int32, (ts, ts), 1)
    valid = col < row
    tric = tric_ref[...]
    trin = trin_ref[...]
    nchunk = past // tc
    for h in range(nh):
        sl = slice(h * HEAD_DIM, (h + 1) * HEAD_DIM)
        q = q_ref[:, sl]
        vc = _head_rows(vc_ref, h, nh, past)
        z2 = _dot_nt(q, kn_ref[:, sl])
        incl2 = _dot(_split2(jnp.where(valid, _softplus2(z2), 0.0)), trin)
        a2 = jnp.where(valid, jnp.exp2(z2 - incl2), 0.0)
        acc = _dot(a2.astype(BF16), vn_ref[:, sl])
        carry = incl2[:, :1]
        z1 = _dot_nt(q, _head_rows(kc_ref, h, nh, past))
        sp1 = _softplus2(z1)
        stacked = jnp.concatenate([sp1[:, c * tc:(c + 1) * tc] for c in range(nchunk)], axis=0)
        incl = _dot(_split2(stacked), tric)
        for c in reversed(range(nchunk)):
            inc_c = incl[c * ts:(c + 1) * ts, :]
            a = jnp.exp2(z1[:, c * tc:(c + 1) * tc] - inc_c - carry)
            acc = acc + _dot(a.astype(BF16), vc[c * tc:(c + 1) * tc, :])
            carry = carry + inc_c[:, :1]
        o_ref[:, sl] = acc


def _sample_specs(ts, past, nh):
    new = pl.BlockSpec((ts, nh * HEAD_DIM), lambda i: (i, 0))
    cache = pl.BlockSpec((None, past * nh, HEAD_DIM), lambda i: (i, 0, 0))
    return new, cache


def _fox_sample(q, kc, vc, kn, vn, ckc, ckn, b, ts, past, h):
    new, cache = _sample_specs(ts, past, h)
    return pl.pallas_call(
        functools.partial(_fox_sample_kernel, nh=h, past=past),
        grid=(b,),
        in_specs=[new, cache, cache, new, new,
                  pl.BlockSpec((None, h, past), lambda i: (i, 0, 0)),
                  pl.BlockSpec((None, h, ts), lambda i: (i, 0, 0))],
        out_specs=new,
        out_shape=jax.ShapeDtypeStruct((b * ts, h * HEAD_DIM), F32),
        compiler_params=_params("parallel"),
        name="fox_sample",
    )(q, kc, vc, kn, vn, ckc, ckn)


def _sb_sample(q, kc, vc, kn, vn, b, ts, past, h):
    new, cache = _sample_specs(ts, past, h)
    tc = _pick(past, (256, 128))
    return pl.pallas_call(
        functools.partial(_sb_sample_kernel, nh=h, past=past, tc=tc),
        grid=(b,),
        in_specs=[new, cache, cache, new, new,
                  pl.BlockSpec((2 * tc, tc), lambda i: (0, 0)),
                  pl.BlockSpec((2 * ts, ts), lambda i: (0, 0))],
        out_specs=new,
        out_shape=jax.ShapeDtypeStruct((b * ts, h * HEAD_DIM), F32),
        compiler_params=_params("parallel"),
        name="sb_sample",
    )(q, kc, vc, kn, vn, _tri2(tc), _tri2(ts))


def _layer(x, b, t, w, caches):
    hf = w["b_forget"].shape[0]
    hs = w["w_sb"][0].shape[1] // HEAD_DIM
    q_scale = HEAD_DIM ** -0.5 * LOG2E
    big = (1024, 512, 256, 128)

    hn = _rmsnorm(x, w["attn_norm"], BF16)
    qkv = {}
    for grp in ("fox", "sb"):
        wq, wk, wv = w["w_" + grp]
        (q,) = _matmul_call(functools.partial(_proj_q_kernel, scale=q_scale), hn, [wq], [], [], [BF16], big,
                            "proj_q_" + grp)
        k32, k16 = _proj_kv(hn, wk, "proj_k_" + grp)
        v32, v16 = _proj_kv(hn, wv, "proj_v_" + grp)
        qkv[grp] = (q, k32, k16, v32, v16)
    (logf_pad,) = _matmul_call(_proj_logf_kernel, hn, [w["w_fl"]], [w["b_fl"]], [], [F32], (LANES,), "proj_logf")

    qf, kf32, kf, vf32, vf = qkv["fox"]
    qs, ks32, ks, vs32, vs = qkv["sb"]
    zeros_init = jnp.zeros((b, 1, LANES), F32)
    if caches is None:
        ct, _ = _cumsum_t(logf_pad.reshape(b, t, LANES), zeros_init)
        o_fox = _fox_prompt(qf, kf, vf, ct[:, :hf, :].reshape(b, hf, 1, t), b, t, hf)
        o_sb = _sb_prompt(qs, ks, vs, b, t, hs)
    else:
        c_fk, c_fv, c_fl, c_sk, c_sv = caches
        past = c_fk.shape[1]
        c_fl_pad = jnp.pad(c_fl.astype(F32), ((0, 0), (0, 0), (0, LANES - hf)))
        ct_c, last = _cumsum_t(c_fl_pad, zeros_init)
        ct_n, _ = _cumsum_t(logf_pad.reshape(b, t, LANES), last)
        o_fox = _fox_sample(qf, c_fk.reshape(b, past * hf, HEAD_DIM), c_fv.reshape(b, past * hf, HEAD_DIM), kf, vf,
                            ct_c[:, :hf, :], ct_n[:, :hf, :], b, t, past, hf)
        o_sb = _sb_sample(qs, c_sk.reshape(b, past * hs, HEAD_DIM), c_sv.reshape(b, past * hs, HEAD_DIM), ks, vs,
                          b, t, past, hs)

    x1, h2, ssq_x1 = _proj_out(o_fox, o_sb, w["out_norm_fox"], w["out_norm_sb"], w["w_out"], x, w["ffn_norm"])
    (act,) = _matmul_call(_proj_swiglu_kernel, h2, [w["w_gate"], w["w_up"]], [], [], [BF16], (256, 128), "ffn_up",
                          tm_candidates=(2048, 1024, 512, 256, 128, 64, 32, 16, 8), row_block_inputs=[ssq_x1])
    (x2,) = _matmul_call(_proj_resid_kernel, act, [w["w_down"]], [], [x1], [F32], (512, 256, 128), "ffn_down",
                         tm_candidates=(512, 256, 128, 64, 32, 16, 8))
    return x2, kf32, vf32, logf_pad[:, :hf], ks32, vs32


def kernel(x_prompt, x_sample, cache_fox_k, cache_fox_v, cache_fox_logf, cache_sb_k, cache_sb_v, attn_norm, w_in,
           b_forget, out_norm_fox, out_norm_sb, w_out, ffn_norm, w_gate, w_up, w_down, final_norm):
    bp, tp, d = x_prompt.shape
    bs, ts, _ = x_sample.shape
    depth = w_in.shape[0]
    hf = cache_fox_k.shape[3]
    hs = cache_sb_k.shape[3]
    w_fox = hf * HEAD_DIM
    w_sb = hs * HEAD_DIM

    xp = x_prompt.reshape(bp * tp, d)
    xs = x_sample.reshape(bs * ts, d)
    outs_p, outs_s = [], []
    for l in range(depth):
        cuts = [0, w_fox, 2 * w_fox, 3 * w_fox, 3 * w_fox + hf, 3 * w_fox + hf + w_sb, 3 * w_fox + hf + 2 * w_sb,
                3 * w_fox + hf + 3 * w_sb]
        seg = _split_cast(w_in, l, cuts, [w_fox] * 3 + [LANES] + [w_sb] * 3)
        w = {
            "attn_norm": attn_norm[l], "ffn_norm": ffn_norm[l],
            "out_norm_fox": out_norm_fox[l], "out_norm_sb": out_norm_sb[l],
            "w_fox": seg[0:3], "w_sb": seg[4:7], "w_fl": seg[3],
            "b_fl": jnp.pad(b_forget[l].astype(F32), (0, LANES - hf)).reshape(1, LANES),
            "b_forget": b_forget[l],
            "w_out": w_out[l].astype(BF16), "w_gate": w_gate[l].astype(BF16), "w_up": w_up[l].astype(BF16),
            "w_down": w_down[l].astype(BF16),
        }
        xp, *rest_p = _layer(xp, bp, tp, w, None)
        caches = (cache_fox_k[l], cache_fox_v[l], cache_fox_logf[l], cache_sb_k[l], cache_sb_v[l])
        xs, *rest_s = _layer(xs, bs, ts, w, caches)
        outs_p.append(rest_p)
        outs_s.append(rest_s)

    y_prompt = _rmsnorm(xp, final_norm, F32).reshape(bp, tp, d)
    y_sample = _rmsnorm(xs, final_norm, F32).reshape(bs, ts, d)

    def stack(outs, idx, shape):
        return jnp.stack([o[idx].reshape(shape) for o in outs])

    res = [y_prompt, y_sample]
    for outs, b, t in ((outs_p, bp, tp), (outs_s, bs, ts)):
        res += [stack(outs, 0, (b, t, hf, HEAD_DIM)), stack(outs, 1, (b, t, hf, HEAD_DIM)),
                stack(outs, 2, (b, t, hf)),
                stack(outs, 3, (b, t, hs, HEAD_DIM)), stack(outs, 4, (b, t, hs, HEAD_DIM))]
    return tuple(res)
```

```python
import functools
import math

import jax
import jax.numpy as jnp
from jax import lax
from jax.experimental import pallas as pl
from jax.experimental.pallas import tpu as pltpu

EPS = 1e-6
NEG_INF = -1e30
HEAD_DIM = 128
LANES = 128
LOG2E = math.log2(math.e)
VMEM_LIMIT_BYTES = 56 * 1024 * 1024
SCORE_BUFFERS = 3

F32 = jnp.float32
BF16 = jnp.bfloat16


def _params(*sem):
    return pltpu.CompilerParams(dimension_semantics=sem, vmem_limit_bytes=VMEM_LIMIT_BYTES)


def _pick(n, candidates):
    for c in candidates:
        if n % c == 0:
            return c
    return n


def _softplus(z):
    return jnp.maximum(z, 0.0) + jnp.log1p(jnp.exp(-jnp.abs(z)))


def _softplus2(z):
    return jnp.maximum(z, 0.0) + jnp.log2(1.0 + jnp.exp2(-jnp.abs(z)))


def _dot(a, b):
    return jnp.dot(a, b, preferred_element_type=F32)


def _dot_nt(a, b):
    return lax.dot_general(a, b, (((1,), (1,)), ((), ())), preferred_element_type=F32)


def _split2(x):
    hi = x.astype(BF16)
    lo = (x - hi.astype(F32)).astype(BF16)
    return jnp.concatenate([hi, lo], axis=1)


def _split3(x):
    hi = x.astype(BF16)
    r = x - hi.astype(F32)
    mid = r.astype(BF16)
    lo = (r - mid.astype(F32)).astype(BF16)
    return hi, mid, lo


def _rmsnorm_kernel(x_ref, g_ref, o_ref):
    x = x_ref[...]
    r = lax.rsqrt(jnp.mean(x * x, axis=-1, keepdims=True) + EPS)
    o_ref[...] = (x * r * g_ref[...]).astype(o_ref.dtype)


def _rmsnorm(x, g, out_dtype):
    m, d = x.shape
    tm = _pick(m, (256, 128, 64, 32, 16, 8))
    return pl.pallas_call(
        _rmsnorm_kernel,
        grid=(m // tm,),
        in_specs=[pl.BlockSpec((tm, d), lambda i: (i, 0)),
                  pl.BlockSpec((1, d), lambda i: (0, 0))],
        out_specs=pl.BlockSpec((tm, d), lambda i: (i, 0)),
        out_shape=jax.ShapeDtypeStruct((m, d), out_dtype),
        compiler_params=_params("parallel"),
        name="rmsnorm",
    )(x, g.reshape(1, d).astype(F32))


def _merge_norm_kernel(a_ref, b_ref, ga_ref, gb_ref, o_ref):
    wa = a_ref.shape[1]
    for ref, g_ref, lo in ((a_ref, ga_ref, 0), (b_ref, gb_ref, wa)):
        x = ref[...]
        r = lax.rsqrt(jnp.mean(x * x, axis=-1, keepdims=True) + EPS)
        o_ref[:, lo:lo + x.shape[1]] = (x * r * g_ref[...]).astype(o_ref.dtype)


def _merge_norm(o_fox, o_sb, g_fox, g_sb):
    m, wa = o_fox.shape
    wb = o_sb.shape[1]
    tm = _pick(m, (256, 128, 64, 32, 16, 8))
    return pl.pallas_call(
        _merge_norm_kernel,
        grid=(m // tm,),
        in_specs=[pl.BlockSpec((tm, wa), lambda i: (i, 0)),
                  pl.BlockSpec((tm, wb), lambda i: (i, 0)),
                  pl.BlockSpec((1, wa), lambda i: (0, 0)),
                  pl.BlockSpec((1, wb), lambda i: (0, 0))],
        out_specs=pl.BlockSpec((tm, wa + wb), lambda i: (i, 0)),
        out_shape=jax.ShapeDtypeStruct((m, wa + wb), BF16),
        compiler_params=_params("parallel"),
        name="merge_norm",
    )(o_fox, o_sb, g_fox.reshape(1, wa).astype(F32), g_sb.reshape(1, wb).astype(F32))


def _proj_q_kernel(x_ref, w_ref, o_ref, *, scale):
    o_ref[...] = (_dot_nt(x_ref[...], w_ref[...]) * scale).astype(o_ref.dtype)


def _proj_kv_kernel(x_ref, w_ref, o32_ref, o16_ref, *, nh):
    acc = _dot_nt(x_ref[...], w_ref[...])
    o16_ref[...] = acc.astype(o16_ref.dtype)
    tm = x_ref.shape[0]
    for h in range(nh):
        o32_ref[pl.ds(h, tm, stride=nh), :] = acc[:, h * HEAD_DIM:(h + 1) * HEAD_DIM]


def _proj_kv(x, w, name):
    m, k = x.shape
    n = w.shape[0]
    nh = n // HEAD_DIM
    tm = _pick(m, (512, 256, 128, 64, 32, 16, 8))
    return pl.pallas_call(
        functools.partial(_proj_kv_kernel, nh=nh),
        grid=(m // tm,),
        in_specs=[pl.BlockSpec((tm, k), lambda i: (i, 0)),
                  pl.BlockSpec((n, k), lambda i: (0, 0))],
        out_specs=[pl.BlockSpec((tm * nh, HEAD_DIM), lambda i: (i, 0)),
                   pl.BlockSpec((tm, n), lambda i: (i, 0))],
        out_shape=[jax.ShapeDtypeStruct((m * nh, HEAD_DIM), F32), jax.ShapeDtypeStruct((m, n), BF16)],
        compiler_params=_params("parallel"),
        name=name,
    )(x, w)


def _proj_logf_kernel(x_ref, w_ref, b_ref, o_ref):
    o_ref[...] = -_softplus(-(_dot_nt(x_ref[...], w_ref[...]) + b_ref[...]))


def _proj_resid_kernel(x_ref, w_ref, r_ref, o_ref):
    o_ref[...] = r_ref[...] + _dot(x_ref[...], w_ref[...])


def _proj_swiglu_kernel(x_ref, wg_ref, wu_ref, o_ref):
    x = x_ref[...]
    g = _dot(x, wg_ref[...])
    u = _dot(x, wu_ref[...])
    o_ref[...] = (g * jax.nn.sigmoid(g) * u).astype(o_ref.dtype)


def _matmul_call(kernel, x, weights, extra_row_inputs, extra_tile_inputs, out_dtypes, tn_candidates, name,
                 tm_candidates=(1024, 512, 256, 128, 64, 32, 16, 8), weights_t=False):
    m, k = x.shape
    n = weights[0].shape[0 if weights_t else 1]
    tm = _pick(m, tm_candidates)
    tn = _pick(n, tn_candidates)
    tile = pl.BlockSpec((tm, tn), lambda i, j: (i, j))
    in_specs = [pl.BlockSpec((tm, k), lambda i, j: (i, 0))]
    w_spec = pl.BlockSpec((tn, k), lambda i, j: (j, 0)) if weights_t else pl.BlockSpec((k, tn), lambda i, j: (0, j))
    in_specs += [w_spec for _ in weights]
    in_specs += [pl.BlockSpec((1, tn), lambda i, j: (0, j)) for _ in extra_row_inputs]
    in_specs += [tile for _ in extra_tile_inputs]
    outs = pl.pallas_call(
        kernel,
        grid=(m // tm, n // tn),
        in_specs=in_specs,
        out_specs=[tile for _ in out_dtypes],
        out_shape=[jax.ShapeDtypeStruct((m, n), dt) for dt in out_dtypes],
        compiler_params=_params("parallel", "arbitrary"),
        name=name,
    )(x, *weights, *extra_row_inputs, *extra_tile_inputs)
    return outs


def _cumsum_kernel(x_ref, init_ref, tril_ref, ct_ref, last_ref, carry_sc):
    t = pl.program_id(1)

    @pl.when(t == 0)
    def _():
        carry_sc[...] = init_ref[...]

    parts = jnp.concatenate(_split3(x_ref[...]), axis=1)
    s = _dot(tril_ref[...], parts)
    cum = s[:, :LANES] + s[:, LANES:2 * LANES] + s[:, 2 * LANES:] + carry_sc[...]
    carry_sc[...] = cum[-1:, :]
    ct_ref[...] = cum.T
    last_ref[...] = cum[-1:, :]


def _cumsum_t(x, init):
    b, t, _ = x.shape
    tt = _pick(t, (512, 256, 128))
    idx = jnp.arange(tt)
    tril = (idx[:, None] >= idx[None, :]).astype(BF16)
    return pl.pallas_call(
        _cumsum_kernel,
        grid=(b, t // tt),
        in_specs=[pl.BlockSpec((None, tt, LANES), lambda i, j: (i, j, 0)),
                  pl.BlockSpec((None, 1, LANES), lambda i, j: (i, 0, 0)),
                  pl.BlockSpec((tt, tt), lambda i, j: (0, 0))],
        out_specs=[pl.BlockSpec((None, LANES, tt), lambda i, j: (i, 0, j)),
                   pl.BlockSpec((None, 1, LANES), lambda i, j: (i, 0, 0))],
        out_shape=[jax.ShapeDtypeStruct((b, LANES, t), F32),
                   jax.ShapeDtypeStruct((b, 1, LANES), F32)],
        scratch_shapes=[pltpu.VMEM((1, LANES), F32)],
        compiler_params=_params("parallel", "arbitrary"),
        name="cumsum_t",
    )(x, init, tril)


def _fox_prompt_kernel(q_ref, k_ref, v_ref, ck_ref, o_ref, kaug_sc, vaug_sc, m_sc, acc_sc, s_sc, *, tq, tk):
    t = k_ref.shape[0]
    nq = t // tq
    nbuf = s_sc.shape[0]
    hi, mid, lo = _split3(ck_ref[...] * LOG2E)
    rows = lax.broadcasted_iota(jnp.int32, (HEAD_DIM, t), 0)
    parts = jnp.where(rows == 0, hi.astype(F32),
                      jnp.where(rows == 1, mid.astype(F32), jnp.where(rows == 2, lo.astype(F32), 0.0)))
    kaug_sc[:, :HEAD_DIM] = k_ref[...]
    kaug_sc[:, HEAD_DIM:] = parts.T.astype(BF16)
    lane = lax.broadcasted_iota(jnp.int32, (t, HEAD_DIM), 1)
    vaug_sc[:, :HEAD_DIM] = v_ref[...]
    vaug_sc[:, HEAD_DIM:] = jnp.where(lane == 0, 1.0, 0.0).astype(BF16)

    lane_q = lax.broadcasted_iota(jnp.int32, (tq, HEAD_DIM), 1)
    q_extra = jnp.where(lane_q < 3, -1.0, 0.0).astype(BF16)
    nc = tk // LANES
    blocks = [(qi, kb) for qi in range(nq) for kb in range((qi * tq) // tk + 1)]

    def scores(n):
        qi, kb = blocks[n]
        qa = jnp.concatenate([q_ref[qi * tq:(qi + 1) * tq, :], q_extra], axis=1)
        s_sc[n % nbuf] = _dot_nt(qa, kaug_sc[kb * tk:(kb + 1) * tk, :])

    def softmax_pv(n):
        qi, kb = blocks[n]
        first = kb == 0
        last = kb == (qi * tq) // tk
        s = s_sc[n % nbuf]
        if last:
            row = qi * tq + lax.broadcasted_iota(jnp.int32, s.shape, 0)
            col = kb * tk + lax.broadcasted_iota(jnp.int32, s.shape, 1)
            s = jnp.where(col <= row, s, NEG_INF)
        chunks = [s[:, c * LANES:(c + 1) * LANES] for c in range(nc)]
        bm = jnp.max(functools.reduce(jnp.maximum, chunks), axis=-1, keepdims=True)
        if first:
            m_new = jnp.broadcast_to(bm, (tq, LANES))
        else:
            m_prev = m_sc[qi]
            m_new = jnp.maximum(m_prev, bm)
            alpha = jnp.exp2(m_prev - m_new)
        p = jnp.concatenate([jnp.exp2(c - m_new).astype(BF16) for c in chunks], axis=1)
        pv = _dot(p, vaug_sc[kb * tk:(kb + 1) * tk, :])
        acc = pv if first else jnp.concatenate([alpha, alpha], axis=1) * acc_sc[qi] + pv
        if last:
            o_ref[qi * tq:(qi + 1) * tq, :] = acc[:, :HEAD_DIM] / acc[:, HEAD_DIM:HEAD_DIM + 1]
        else:
            acc_sc[qi] = acc
            m_sc[qi] = m_new

    scores(0)
    for n in range(len(blocks)):
        if n + 1 < len(blocks):
            scores(n + 1)
        softmax_pv(n)


def _fox_prompt(q, k, v, ck, b, t, h):
    tq = _pick(t, (512, 256, 128))
    nq = t // tq
    blk = pl.BlockSpec((t, HEAD_DIM), lambda i, j: (i, j))
    return pl.pallas_call(
        functools.partial(_fox_prompt_kernel, tq=tq, tk=tq),
        grid=(b, h),
        in_specs=[blk, blk, blk, pl.BlockSpec((None, None, 1, t), lambda i, j: (i, j, 0, 0))],
        out_specs=blk,
        out_shape=jax.ShapeDtypeStruct((b * t, h * HEAD_DIM), F32),
        scratch_shapes=[pltpu.VMEM((t, 2 * HEAD_DIM), BF16), pltpu.VMEM((t, 2 * HEAD_DIM), BF16),
                        pltpu.VMEM((nq, tq, LANES), F32), pltpu.VMEM((nq, tq, 2 * HEAD_DIM), F32),
                        pltpu.VMEM((SCORE_BUFFERS, tq, tq), F32)],
        compiler_params=_params("parallel", "parallel"),
        name="fox_prompt",
    )(q, k, v, ck)


def _tri2(tk):
    idx = jnp.arange(tk)
    tri = (idx[:, None] >= idx[None, :]).astype(BF16)
    return jnp.concatenate([tri, tri], axis=0)


SKIP_LOG2 = 152.0


def _sb_prompt_kernel(q_ref, k_ref, v_ref, tri2_ref, o_ref, carry_sc, acc_sc, z_sc, *, tb):
    t = k_ref.shape[0]
    nq = t // tb
    nbuf = z_sc.shape[0]
    tri2 = tri2_ref[...]
    nc = tb // LANES

    def q_tile(qi):
        return q_ref[qi * tb:(qi + 1) * tb, :]

    def weights(z, carry, diagonal):
        sp = _softplus2(z)
        if diagonal:
            row = lax.broadcasted_iota(jnp.int32, z.shape, 0)
            col = lax.broadcasted_iota(jnp.int32, z.shape, 1)
            valid = col < row
            sp = jnp.where(valid, sp, 0.0)
        incl = _dot(_split2(sp), tri2)
        arg = z - incl
        if carry is not None:
            arg = arg - jnp.concatenate([carry] * nc, axis=1)
        a = jnp.exp2(arg)
        if diagonal:
            a = jnp.where(valid, a, 0.0)
        tot = incl[:, :1]
        new_carry = jnp.broadcast_to(tot, (tb, LANES)) if carry is None else carry + tot
        return a.astype(BF16), new_carry

    blocks = [(qi, kb) for qi in range(nq) for kb in (qi, qi - 1) if kb >= 0]

    def scores(n):
        qi, kb = blocks[n]
        z_sc[n % nbuf] = _dot_nt(q_tile(qi), k_ref[kb * tb:(kb + 1) * tb, :])

    def near(n):
        qi, kb = blocks[n]
        first = kb == qi
        a, carry = weights(z_sc[n % nbuf], None if first else carry_sc[qi], first)
        pv = _dot(a, v_ref[kb * tb:(kb + 1) * tb, :])
        acc = pv if first else acc_sc[qi] + pv
        if kb == 0:
            o_ref[qi * tb:(qi + 1) * tb, :] = acc
        else:
            acc_sc[qi] = acc
            carry_sc[qi] = carry

    scores(0)
    for n in range(len(blocks)):
        if n + 1 < len(blocks):
            scores(n + 1)
        near(n)

    for qi in range(2, nq):
        def more(kb):
            return jnp.logical_and(kb >= 0, jnp.min(carry_sc[qi]) <= SKIP_LOG2)

        def far(kb):
            start = pl.multiple_of(kb * tb, tb)
            z = _dot_nt(q_tile(qi), k_ref[pl.ds(start, tb), :])
            a, carry = weights(z, carry_sc[qi], False)
            acc_sc[qi] = acc_sc[qi] + _dot(a, v_ref[pl.ds(start, tb), :])
            carry_sc[qi] = carry
            return kb - 1

        lax.while_loop(more, far, qi - 2)
        o_ref[qi * tb:(qi + 1) * tb, :] = acc_sc[qi]


def _sb_prompt(q, k, v, b, t, h):
    tb = _pick(t, (256, 128))
    nq = t // tb
    blk = pl.BlockSpec((t, HEAD_DIM), lambda i, j: (i, j))
    return pl.pallas_call(
        functools.partial(_sb_prompt_kernel, tb=tb),
        grid=(b, h),
        in_specs=[blk, blk, blk, pl.BlockSpec((2 * tb, tb), lambda i, j: (0, 0))],
        out_specs=blk,
        out_shape=jax.ShapeDtypeStruct((b * t, h * HEAD_DIM), F32),
        scratch_shapes=[pltpu.VMEM((nq, tb, LANES), F32), pltpu.VMEM((nq, tb, HEAD_DIM), F32),
                        pltpu.VMEM((SCORE_BUFFERS, tb, tb), F32)],
        compiler_params=_params("parallel", "parallel"),
        name="sb_prompt",
    )(q, k, v, _tri2(tb))


def _head_rows(ref, h, nh, past):
    return ref[pl.ds(h, past, stride=nh), :].astype(BF16)


def _fox_sample_kernel(q_ref, kc_ref, vc_ref, kn_ref, vn_ref, ckc_ref, ckn_ref, o_ref, *, nh, past):
    ts = q_ref.shape[0]
    row = lax.broadcasted_iota(jnp.int32, (ts, ts), 0)
    col = lax.broadcasted_iota(jnp.int32, (ts, ts), 1)
    for h in range(nh):
        sl = slice(h * HEAD_DIM, (h + 1) * HEAD_DIM)
        q = q_ref[:, sl]
        s1 = _dot_nt(q, _head_rows(kc_ref, h, nh, past)) - ckc_ref[h:h + 1, :] * LOG2E
        s2 = _dot_nt(q, kn_ref[:, sl]) - ckn_ref[h:h + 1, :] * LOG2E
        s2 = jnp.where(col <= row, s2, NEG_INF)
        m = jnp.maximum(jnp.max(s1, axis=-1, keepdims=True), jnp.max(s2, axis=-1, keepdims=True))
        p1 = jnp.exp2(s1 - m)
        p2 = jnp.exp2(s2 - m)
        l = jnp.sum(p1, axis=-1, keepdims=True) + jnp.sum(p2, axis=-1, keepdims=True)
        acc = _dot(p1.astype(BF16), _head_rows(vc_ref, h, nh, past)) + _dot(p2.astype(BF16), vn_ref[:, sl])
        o_ref[:, sl] = acc / l


def _sb_sample_kernel(q_ref, kc_ref, vc_ref, kn_ref, vn_ref, tric_ref, trin_ref, o_ref, *, nh, past, tc):
    ts = q_ref.shape[0]
    row = lax.broadcasted_iota(jnp.int32, (ts, ts), 0)
    col = lax.broadcasted_iota(jnp.int32, (ts, ts), 1)
    valid = col < row
    tric = tric_ref[...]
    trin = trin_ref[...]
    nchunk = past // tc
    for h in range(nh):
        sl = slice(h * HEAD_DIM, (h + 1) * HEAD_DIM)
        q = q_ref[:, sl]
        vc = _head_rows(vc_ref, h, nh, past)
        z2 = _dot_nt(q, kn_ref[:, sl])
        incl2 = _dot(_split2(jnp.where(valid, _softplus2(z2), 0.0)), trin)
        a2 = jnp.where(valid, jnp.exp2(z2 - incl2), 0.0)
        acc = _dot(a2.astype(BF16), vn_ref[:, sl])
        carry = incl2[:, :1]
        z1 = _dot_nt(q, _head_rows(kc_ref, h, nh, past))
        sp1 = _softplus2(z1)
        stacked = jnp.concatenate([sp1[:, c * tc:(c + 1) * tc] for c in range(nchunk)], axis=0)
        incl = _dot(_split2(stacked), tric)
        for c in reversed(range(nchunk)):
            inc_c = incl[c * ts:(c + 1) * ts, :]
            a = jnp.exp2(z1[:, c * tc:(c + 1) * tc] - inc_c - carry)
            acc = acc + _dot(a.astype(BF16), vc[c * tc:(c + 1) * tc, :])
            carry = carry + inc_c[:, :1]
        o_ref[:, sl] = acc


def _sample_specs(ts, past, nh):
    new = pl.BlockSpec((ts, nh * HEAD_DIM), lambda i: (i, 0))
    cache = pl.BlockSpec((None, past * nh, HEAD_DIM), lambda i: (i, 0, 0))
    return new, cache


def _fox_sample(q, kc, vc, kn, vn, ckc, ckn, b, ts, past, h):
    new, cache = _sample_specs(ts, past, h)
    return pl.pallas_call(
        functools.partial(_fox_sample_kernel, nh=h, past=past),
        grid=(b,),
        in_specs=[new, cache, cache, new, new,
                  pl.BlockSpec((None, h, past), lambda i: (i, 0, 0)),
                  pl.BlockSpec((None, h, ts), lambda i: (i, 0, 0))],
        out_specs=new,
        out_shape=jax.ShapeDtypeStruct((b * ts, h * HEAD_DIM), F32),
        compiler_params=_params("parallel"),
        name="fox_sample",
    )(q, kc, vc, kn, vn, ckc, ckn)


def _sb_sample(q, kc, vc, kn, vn, b, ts, past, h):
    new, cache = _sample_specs(ts, past, h)
    tc = _pick(past, (256, 128))
    return pl.pallas_call(
        functools.partial(_sb_sample_kernel, nh=h, past=past, tc=tc),
        grid=(b,),
        in_specs=[new, cache, cache, new, new,
                  pl.BlockSpec((2 * tc, tc), lambda i: (0, 0)),
                  pl.BlockSpec((2 * ts, ts), lambda i: (0, 0))],
        out_specs=new,
        out_shape=jax.ShapeDtypeStruct((b * ts, h * HEAD_DIM), F32),
        compiler_params=_params("parallel"),
        name="sb_sample",
    )(q, kc, vc, kn, vn, _tri2(tc), _tri2(ts))


def _layer(x, b, t, w, caches):
    hf = w["b_forget"].shape[0]
    hs = w["w_sb"][0].shape[0] // HEAD_DIM
    q_scale = HEAD_DIM ** -0.5 * LOG2E
    big = (1024, 512, 256, 128)

    hn = _rmsnorm(x, w["attn_norm"], BF16)
    qkv = {}
    for grp in ("fox", "sb"):
        wq, wk, wv = w["w_" + grp]
        (q,) = _matmul_call(functools.partial(_proj_q_kernel, scale=q_scale), hn, [wq], [], [], [BF16], big,
                            "proj_q_" + grp, weights_t=True)
        k32, k16 = _proj_kv(hn, wk, "proj_k_" + grp)
        v32, v16 = _proj_kv(hn, wv, "proj_v_" + grp)
        qkv[grp] = (q, k32, k16, v32, v16)
    (logf_pad,) = _matmul_call(_proj_logf_kernel, hn, [w["w_fl"]], [w["b_fl"]], [], [F32], (LANES,), "proj_logf",
                               weights_t=True)

    qf, kf32, kf, vf32, vf = qkv["fox"]
    qs, ks32, ks, vs32, vs = qkv["sb"]
    zeros_init = jnp.zeros((b, 1, LANES), F32)
    if caches is None:
        ct, _ = _cumsum_t(logf_pad.reshape(b, t, LANES), zeros_init)
        o_fox = _fox_prompt(qf, kf, vf, ct[:, :hf, :].reshape(b, hf, 1, t), b, t, hf)
        o_sb = _sb_prompt(qs, ks, vs, b, t, hs)
    else:
        c_fk, c_fv, c_fl, c_sk, c_sv = caches
        past = c_fk.shape[1]
        c_fl_pad = jnp.pad(c_fl.astype(F32), ((0, 0), (0, 0), (0, LANES - hf)))
        ct_c, last = _cumsum_t(c_fl_pad, zeros_init)
        ct_n, _ = _cumsum_t(logf_pad.reshape(b, t, LANES), last)
        o_fox = _fox_sample(qf, c_fk.reshape(b, past * hf, HEAD_DIM), c_fv.reshape(b, past * hf, HEAD_DIM), kf, vf,
                            ct_c[:, :hf, :], ct_n[:, :hf, :], b, t, past, hf)
        o_sb = _sb_sample(qs, c_sk.reshape(b, past * hs, HEAD_DIM), c_sv.reshape(b, past * hs, HEAD_DIM), ks, vs,
                          b, t, past, hs)

    a = _merge_norm(o_fox, o_sb, w["out_norm_fox"], w["out_norm_sb"])
    (x1,) = _matmul_call(_proj_resid_kernel, a, [w["w_out"]], [], [x], [F32], big, "proj_out")
    h2 = _rmsnorm(x1, w["ffn_norm"], BF16)
    (act,) = _matmul_call(_proj_swiglu_kernel, h2, [w["w_gate"], w["w_up"]], [], [], [BF16], (256, 128), "ffn_up",
                          tm_candidates=(2048, 1024, 512, 256, 128, 64, 32, 16, 8))
    (x2,) = _matmul_call(_proj_resid_kernel, act, [w["w_down"]], [], [x1], [F32], (512, 256, 128), "ffn_down",
                         tm_candidates=(512, 256, 128, 64, 32, 16, 8))
    return x2, kf32, vf32, logf_pad[:, :hf], ks32, vs32


def kernel(x_prompt, x_sample, cache_fox_k, cache_fox_v, cache_fox_logf, cache_sb_k, cache_sb_v, attn_norm, w_in,
           b_forget, out_norm_fox, out_norm_sb, w_out, ffn_norm, w_gate, w_up, w_down, final_norm):
    bp, tp, d = x_prompt.shape
    bs, ts, _ = x_sample.shape
    depth = w_in.shape[0]
    hf = cache_fox_k.shape[3]
    hs = cache_sb_k.shape[3]
    w_fox = hf * HEAD_DIM
    w_sb = hs * HEAD_DIM

    xp = x_prompt.reshape(bp * tp, d)
    xs = x_sample.reshape(bs * ts, d)
    outs_p, outs_s = [], []
    for l in range(depth):
        cuts = [0, w_fox, 2 * w_fox, 3 * w_fox, 3 * w_fox + hf, 3 * w_fox + hf + w_sb, 3 * w_fox + hf + 2 * w_sb,
                3 * w_fox + hf + 3 * w_sb]
        wl = jnp.swapaxes(w_in[l], 0, 1)
        seg = [wl[cuts[i]:cuts[i + 1], :] for i in range(7)]
        w = {
            "attn_norm": attn_norm[l], "ffn_norm": ffn_norm[l],
            "out_norm_fox": out_norm_fox[l], "out_norm_sb": out_norm_sb[l],
            "w_fox": [s.astype(BF16) for s in seg[0:3]],
            "w_sb": [s.astype(BF16) for s in seg[4:7]],
            "w_fl": jnp.pad(seg[3], ((0, LANES - hf), (0, 0))).astype(BF16),
            "b_fl": jnp.pad(b_forget[l].astype(F32), (0, LANES - hf)).reshape(1, LANES),
            "b_forget": b_forget[l],
            "w_out": w_out[l].astype(BF16), "w_gate": w_gate[l].astype(BF16), "w_up": w_up[l].astype(BF16),
            "w_down": w_down[l].astype(BF16),
        }
        xp, *rest_p = _layer(xp, bp, tp, w, None)
        caches = (cache_fox_k[l], cache_fox_v[l], cache_fox_logf[l], cache_sb_k[l], cache_sb_v[l])
        xs, *rest_s = _layer(xs, bs, ts, w, caches)
        outs_p.append(rest_p)
        outs_s.append(rest_s)

    y_prompt = _rmsnorm(xp, final_norm, F32).reshape(bp, tp, d)
    y_sample = _rmsnorm(xs, final_norm, F32).reshape(bs, ts, d)

    def stack(outs, idx, shape):
        return jnp.stack([o[idx].reshape(shape) for o in outs])

    res = [y_prompt, y_sample]
    for outs, b, t in ((outs_p, bp, tp), (outs_s, bs, ts)):
        res += [stack(outs, 0, (b, t, hf, HEAD_DIM)), stack(outs, 1, (b, t, hf, HEAD_DIM)),
                stack(outs, 2, (b, t, hf)),
                stack(outs, 3, (b, t, hs, HEAD_DIM)), stack(outs, 4, (b, t, hs, HEAD_DIM))]
    return tuple(res)
```

```python
import functools
import math

import jax
import jax.numpy as jnp
from jax import lax
from jax.experimental import pallas as pl
from jax.experimental.pallas import tpu as pltpu

EPS = 1e-6
NEG_INF = -1e30
HEAD_DIM = 128
LANES = 128
LOG2E = math.log2(math.e)
VMEM_LIMIT_BYTES = 56 * 1024 * 1024
SCORE_BUFFERS = 3

F32 = jnp.float32
BF16 = jnp.bfloat16


def _params(*sem):
    return pltpu.CompilerParams(dimension_semantics=sem, vmem_limit_bytes=VMEM_LIMIT_BYTES)


def _pick(n, candidates):
    for c in candidates:
        if n % c == 0:
            return c
    return n


def _softplus(z):
    return jnp.maximum(z, 0.0) + jnp.log1p(jnp.exp(-jnp.abs(z)))


def _softplus2(z):
    return jnp.maximum(z, 0.0) + jnp.log2(1.0 + jnp.exp2(-jnp.abs(z)))


def _dot(a, b):
    return jnp.dot(a, b, preferred_element_type=F32)


def _dot_nt(a, b):
    return lax.dot_general(a, b, (((1,), (1,)), ((), ())), preferred_element_type=F32)


def _split2(x):
    hi = x.astype(BF16)
    lo = (x - hi.astype(F32)).astype(BF16)
    return jnp.concatenate([hi, lo], axis=1)


def _split3(x):
    hi = x.astype(BF16)
    r = x - hi.astype(F32)
    mid = r.astype(BF16)
    lo = (r - mid.astype(F32)).astype(BF16)
    return hi, mid, lo


def _rmsnorm_kernel(x_ref, g_ref, o_ref):
    x = x_ref[...]
    r = lax.rsqrt(jnp.mean(x * x, axis=-1, keepdims=True) + EPS)
    o_ref[...] = (x * r * g_ref[...]).astype(o_ref.dtype)


def _rmsnorm(x, g, out_dtype):
    m, d = x.shape
    tm = _pick(m, (256, 128, 64, 32, 16, 8))
    return pl.pallas_call(
        _rmsnorm_kernel,
        grid=(m // tm,),
        in_specs=[pl.BlockSpec((tm, d), lambda i: (i, 0)),
                  pl.BlockSpec((1, d), lambda i: (0, 0))],
        out_specs=pl.BlockSpec((tm, d), lambda i: (i, 0)),
        out_shape=jax.ShapeDtypeStruct((m, d), out_dtype),
        compiler_params=_params("parallel"),
        name="rmsnorm",
    )(x, g.reshape(1, d).astype(F32))


def _merge_norm_kernel(a_ref, b_ref, ga_ref, gb_ref, o_ref):
    wa = a_ref.shape[1]
    for ref, g_ref, lo in ((a_ref, ga_ref, 0), (b_ref, gb_ref, wa)):
        x = ref[...]
        r = lax.rsqrt(jnp.mean(x * x, axis=-1, keepdims=True) + EPS)
        o_ref[:, lo:lo + x.shape[1]] = (x * r * g_ref[...]).astype(o_ref.dtype)


def _merge_norm(o_fox, o_sb, g_fox, g_sb):
    m, wa = o_fox.shape
    wb = o_sb.shape[1]
    tm = _pick(m, (256, 128, 64, 32, 16, 8))
    return pl.pallas_call(
        _merge_norm_kernel,
        grid=(m // tm,),
        in_specs=[pl.BlockSpec((tm, wa), lambda i: (i, 0)),
                  pl.BlockSpec((tm, wb), lambda i: (i, 0)),
                  pl.BlockSpec((1, wa), lambda i: (0, 0)),
                  pl.BlockSpec((1, wb), lambda i: (0, 0))],
        out_specs=pl.BlockSpec((tm, wa + wb), lambda i: (i, 0)),
        out_shape=jax.ShapeDtypeStruct((m, wa + wb), BF16),
        compiler_params=_params("parallel"),
        name="merge_norm",
    )(o_fox, o_sb, g_fox.reshape(1, wa).astype(F32), g_sb.reshape(1, wb).astype(F32))


def _proj_q_kernel(x_ref, w_ref, o_ref, *, scale):
    o_ref[...] = (_dot_nt(x_ref[...], w_ref[...]) * scale).astype(o_ref.dtype)


def _proj_kv_kernel(x_ref, w_ref, o32_ref, o16_ref, *, nh):
    acc = _dot_nt(x_ref[...], w_ref[...])
    o16_ref[...] = acc.astype(o16_ref.dtype)
    tm = x_ref.shape[0]
    for h in range(nh):
        o32_ref[pl.ds(h, tm, stride=nh), :] = acc[:, h * HEAD_DIM:(h + 1) * HEAD_DIM]


def _proj_kv(x, w, name):
    m, k = x.shape
    n = w.shape[0]
    nh = n // HEAD_DIM
    tm = _pick(m, (512, 256, 128, 64, 32, 16, 8))
    return pl.pallas_call(
        functools.partial(_proj_kv_kernel, nh=nh),
        grid=(m // tm,),
        in_specs=[pl.BlockSpec((tm, k), lambda i: (i, 0)),
                  pl.BlockSpec((n, k), lambda i: (0, 0))],
        out_specs=[pl.BlockSpec((tm * nh, HEAD_DIM), lambda i: (i, 0)),
                   pl.BlockSpec((tm, n), lambda i: (i, 0))],
        out_shape=[jax.ShapeDtypeStruct((m * nh, HEAD_DIM), F32), jax.ShapeDtypeStruct((m, n), BF16)],
        compiler_params=_params("parallel"),
        name=name,
    )(x, w)


def _proj_logf_kernel(x_ref, w_ref, b_ref, o_ref):
    o_ref[...] = -_softplus(-(_dot_nt(x_ref[...], w_ref[...]) + b_ref[...]))


def _proj_resid_kernel(x_ref, w_ref, r_ref, o_ref):
    o_ref[...] = r_ref[...] + _dot(x_ref[...], w_ref[...])


def _proj_swiglu_kernel(x_ref, wg_ref, wu_ref, o_ref):
    x = x_ref[...]
    g = _dot(x, wg_ref[...])
    u = _dot(x, wu_ref[...])
    o_ref[...] = (g * jax.nn.sigmoid(g) * u).astype(o_ref.dtype)


def _matmul_call(kernel, x, weights, extra_row_inputs, extra_tile_inputs, out_dtypes, tn_candidates, name,
                 tm_candidates=(1024, 512, 256, 128, 64, 32, 16, 8), weights_t=False):
    m, k = x.shape
    n = weights[0].shape[0 if weights_t else 1]
    tm = _pick(m, tm_candidates)
    tn = _pick(n, tn_candidates)
    tile = pl.BlockSpec((tm, tn), lambda i, j: (i, j))
    in_specs = [pl.BlockSpec((tm, k), lambda i, j: (i, 0))]
    w_spec = pl.BlockSpec((tn, k), lambda i, j: (j, 0)) if weights_t else pl.BlockSpec((k, tn), lambda i, j: (0, j))
    in_specs += [w_spec for _ in weights]
    in_specs += [pl.BlockSpec((1, tn), lambda i, j: (0, j)) for _ in extra_row_inputs]
    in_specs += [tile for _ in extra_tile_inputs]
    outs = pl.pallas_call(
        kernel,
        grid=(m // tm, n // tn),
        in_specs=in_specs,
        out_specs=[tile for _ in out_dtypes],
        out_shape=[jax.ShapeDtypeStruct((m, n), dt) for dt in out_dtypes],
        compiler_params=_params("parallel", "arbitrary"),
        name=name,
    )(x, *weights, *extra_row_inputs, *extra_tile_inputs)
    return outs


def _cumsum_kernel(x_ref, init_ref, tril_ref, ct_ref, last_ref, carry_sc):
    t = pl.program_id(1)

    @pl.when(t == 0)
    def _():
        carry_sc[...] = init_ref[...]

    parts = jnp.concatenate(_split3(x_ref[...]), axis=1)
    s = _dot(tril_ref[...], parts)
    cum = s[:, :LANES] + s[:, LANES:2 * LANES] + s[:, 2 * LANES:] + carry_sc[...]
    carry_sc[...] = cum[-1:, :]
    ct_ref[...] = cum.T
    last_ref[...] = cum[-1:, :]


def _cumsum_t(x, init):
    b, t, _ = x.shape
    tt = _pick(t, (512, 256, 128))
    idx = jnp.arange(tt)
    tril = (idx[:, None] >= idx[None, :]).astype(BF16)
    return pl.pallas_call(
        _cumsum_kernel,
        grid=(b, t // tt),
        in_specs=[pl.BlockSpec((None, tt, LANES), lambda i, j: (i, j, 0)),
                  pl.BlockSpec((None, 1, LANES), lambda i, j: (i, 0, 0)),
                  pl.BlockSpec((tt, tt), lambda i, j: (0, 0))],
        out_specs=[pl.BlockSpec((None, LANES, tt), lambda i, j: (i, 0, j)),
                   pl.BlockSpec((None, 1, LANES), lambda i, j: (i, 0, 0))],
        out_shape=[jax.ShapeDtypeStruct((b, LANES, t), F32),
                   jax.ShapeDtypeStruct((b, 1, LANES), F32)],
        scratch_shapes=[pltpu.VMEM((1, LANES), F32)],
        compiler_params=_params("parallel", "arbitrary"),
        name="cumsum_t",
    )(x, init, tril)


def _side_rows(k, steps):
    for rows in range(16, k + 1, 16):
        if k % rows == 0 and k // rows <= steps:
            return rows
    return None


def _side_specs(weights, steps, step_index):
    specs, shapes = [], []
    for w in weights:
        rows = _side_rows(w.shape[0], steps)
        last = w.shape[0] // rows - 1
        specs.append(pl.BlockSpec((rows, w.shape[1]),
                                  lambda *g, last=last: (jnp.minimum(step_index(*g), last), 0)))
        shapes.append(jax.ShapeDtypeStruct(w.shape, BF16))
    return specs, shapes


def _side_cast(refs):
    n = len(refs) // 2
    for w_ref, o_ref in zip(refs[:n], refs[n:]):
        o_ref[...] = w_ref[...].astype(o_ref.dtype)


def _fox_prompt_kernel(q_ref, k_ref, v_ref, ck_ref, *rest, tq, tk, n_side):
    side_in, (o_ref, *side_out) = rest[:n_side], rest[n_side:2 * n_side + 1]
    kaug_sc, vaug_sc, m_sc, acc_sc, s_sc = rest[2 * n_side + 1:]
    _side_cast(list(side_in) + side_out)
    t = k_ref.shape[0]
    nq = t // tq
    nbuf = s_sc.shape[0]
    hi, mid, lo = _split3(ck_ref[...] * LOG2E)
    rows = lax.broadcasted_iota(jnp.int32, (HEAD_DIM, t), 0)
    parts = jnp.where(rows == 0, hi.astype(F32),
                      jnp.where(rows == 1, mid.astype(F32), jnp.where(rows == 2, lo.astype(F32), 0.0)))
    kaug_sc[:, :HEAD_DIM] = k_ref[...]
    kaug_sc[:, HEAD_DIM:] = parts.T.astype(BF16)
    lane = lax.broadcasted_iota(jnp.int32, (t, HEAD_DIM), 1)
    vaug_sc[:, :HEAD_DIM] = v_ref[...]
    vaug_sc[:, HEAD_DIM:] = jnp.where(lane == 0, 1.0, 0.0).astype(BF16)

    lane_q = lax.broadcasted_iota(jnp.int32, (tq, HEAD_DIM), 1)
    q_extra = jnp.where(lane_q < 3, -1.0, 0.0).astype(BF16)
    nc = tk // LANES
    blocks = [(qi, kb) for qi in range(nq) for kb in range((qi * tq) // tk + 1)]

    def scores(n):
        qi, kb = blocks[n]
        qa = jnp.concatenate([q_ref[qi * tq:(qi + 1) * tq, :], q_extra], axis=1)
        s_sc[n % nbuf] = _dot_nt(qa, kaug_sc[kb * tk:(kb + 1) * tk, :])

    def softmax_pv(n):
        qi, kb = blocks[n]
        first = kb == 0
        last = kb == (qi * tq) // tk
        s = s_sc[n % nbuf]
        if last:
            row = qi * tq + lax.broadcasted_iota(jnp.int32, s.shape, 0)
            col = kb * tk + lax.broadcasted_iota(jnp.int32, s.shape, 1)
            s = jnp.where(col <= row, s, NEG_INF)
        chunks = [s[:, c * LANES:(c + 1) * LANES] for c in range(nc)]
        bm = jnp.max(functools.reduce(jnp.maximum, chunks), axis=-1, keepdims=True)
        if first:
            m_new = jnp.broadcast_to(bm, (tq, LANES))
        else:
            m_prev = m_sc[qi]
            m_new = jnp.maximum(m_prev, bm)
            alpha = jnp.exp2(m_prev - m_new)
        p = jnp.concatenate([jnp.exp2(c - m_new).astype(BF16) for c in chunks], axis=1)
        pv = _dot(p, vaug_sc[kb * tk:(kb + 1) * tk, :])
        acc = pv if first else jnp.concatenate([alpha, alpha], axis=1) * acc_sc[qi] + pv
        if last:
            o_ref[qi * tq:(qi + 1) * tq, :] = acc[:, :HEAD_DIM] / acc[:, HEAD_DIM:HEAD_DIM + 1]
        else:
            acc_sc[qi] = acc
            m_sc[qi] = m_new

    scores(0)
    for n in range(len(blocks)):
        if n + 1 < len(blocks):
            scores(n + 1)
        softmax_pv(n)


def _fox_prompt(q, k, v, ck, side, b, t, h):
    tq = _pick(t, (512, 256, 128))
    nq = t // tq
    blk = pl.BlockSpec((t, HEAD_DIM), lambda i, j: (i, j))
    side_specs, side_shapes = _side_specs(side, b * h, lambda i, j: i * h + j)
    o, *cast = pl.pallas_call(
        functools.partial(_fox_prompt_kernel, tq=tq, tk=tq, n_side=len(side)),
        grid=(b, h),
        in_specs=[blk, blk, blk, pl.BlockSpec((None, None, 1, t), lambda i, j: (i, j, 0, 0))] + side_specs,
        out_specs=[blk] + side_specs,
        out_shape=[jax.ShapeDtypeStruct((b * t, h * HEAD_DIM), F32)] + side_shapes,
        scratch_shapes=[pltpu.VMEM((t, 2 * HEAD_DIM), BF16), pltpu.VMEM((t, 2 * HEAD_DIM), BF16),
                        pltpu.VMEM((nq, tq, LANES), F32), pltpu.VMEM((nq, tq, 2 * HEAD_DIM), F32),
                        pltpu.VMEM((SCORE_BUFFERS, tq, tq), F32)],
        compiler_params=_params("arbitrary", "arbitrary"),
        name="fox_prompt",
    )(q, k, v, ck, *side)
    return o, cast


def _tri2(tk):
    idx = jnp.arange(tk)
    tri = (idx[:, None] >= idx[None, :]).astype(BF16)
    return jnp.concatenate([tri, tri], axis=0)


SKIP_LOG2 = 152.0


def _sb_prompt_kernel(q_ref, k_ref, v_ref, tri2_ref, *rest, tb, n_side):
    side_in, (o_ref, *side_out) = rest[:n_side], rest[n_side:2 * n_side + 1]
    carry_sc, acc_sc, z_sc = rest[2 * n_side + 1:]
    _side_cast(list(side_in) + side_out)
    t = k_ref.shape[0]
    nq = t // tb
    nbuf = z_sc.shape[0]
    tri2 = tri2_ref[...]
    nc = tb // LANES

    def q_tile(qi):
        return q_ref[qi * tb:(qi + 1) * tb, :]

    def weights(z, carry, diagonal):
        sp = _softplus2(z)
        if diagonal:
            row = lax.broadcasted_iota(jnp.int32, z.shape, 0)
            col = lax.broadcasted_iota(jnp.int32, z.shape, 1)
            valid = col < row
            sp = jnp.where(valid, sp, 0.0)
        incl = _dot(_split2(sp), tri2)
        arg = z - incl
        if carry is not None:
            arg = arg - jnp.concatenate([carry] * nc, axis=1)
        a = jnp.exp2(arg)
        if diagonal:
            a = jnp.where(valid, a, 0.0)
        tot = incl[:, :1]
        new_carry = jnp.broadcast_to(tot, (tb, LANES)) if carry is None else carry + tot
        return a.astype(BF16), new_carry

    blocks = [(qi, kb) for qi in range(nq) for kb in (qi, qi - 1) if kb >= 0]

    def scores(n):
        qi, kb = blocks[n]
        z_sc[n % nbuf] = _dot_nt(q_tile(qi), k_ref[kb * tb:(kb + 1) * tb, :])

    def near(n):
        qi, kb = blocks[n]
        first = kb == qi
        a, carry = weights(z_sc[n % nbuf], None if first else carry_sc[qi], first)
        pv = _dot(a, v_ref[kb * tb:(kb + 1) * tb, :])
        acc = pv if first else acc_sc[qi] + pv
        if kb == qi - 1 or kb == 0:
            o_ref[qi * tb:(qi + 1) * tb, :] = acc
        if kb > 0:
            acc_sc[qi] = acc
            carry_sc[qi] = carry

    scores(0)
    for n in range(len(blocks)):
        if n + 1 < len(blocks):
            scores(n + 1)
        near(n)

    if nq <= 2:
        return
    lightest = functools.reduce(jnp.minimum, [carry_sc[qi] for qi in range(2, nq)])

    @pl.when(jnp.min(lightest) <= SKIP_LOG2)
    def _():
        for qi in range(2, nq):
            def more(kb):
                return jnp.logical_and(kb >= 0, jnp.min(carry_sc[qi]) <= SKIP_LOG2)

            def far(kb):
                start = pl.multiple_of(kb * tb, tb)
                z = _dot_nt(q_tile(qi), k_ref[pl.ds(start, tb), :])
                a, carry = weights(z, carry_sc[qi], False)
                acc_sc[qi] = acc_sc[qi] + _dot(a, v_ref[pl.ds(start, tb), :])
                carry_sc[qi] = carry
                return kb - 1

            lax.while_loop(more, far, qi - 2)
            o_ref[qi * tb:(qi + 1) * tb, :] = acc_sc[qi]


def _sb_prompt(q, k, v, side, b, t, h):
    tb = _pick(t, (256, 128))
    nq = t // tb
    blk = pl.BlockSpec((t, HEAD_DIM), lambda i, j: (i, j))
    side_specs, side_shapes = _side_specs(side, b * h, lambda i, j: i * h + j)
    o, *cast = pl.pallas_call(
        functools.partial(_sb_prompt_kernel, tb=tb, n_side=len(side)),
        grid=(b, h),
        in_specs=[blk, blk, blk, pl.BlockSpec((2 * tb, tb), lambda i, j: (0, 0))] + side_specs,
        out_specs=[blk] + side_specs,
        out_shape=[jax.ShapeDtypeStruct((b * t, h * HEAD_DIM), F32)] + side_shapes,
        scratch_shapes=[pltpu.VMEM((nq, tb, LANES), F32), pltpu.VMEM((nq, tb, HEAD_DIM), F32),
                        pltpu.VMEM((SCORE_BUFFERS, tb, tb), F32)],
        compiler_params=_params("arbitrary", "arbitrary"),
        name="sb_prompt",
    )(q, k, v, _tri2(tb), *side)
    return o, cast


def _head_rows(ref, h, nh, past):
    return ref[pl.ds(h, past, stride=nh), :].astype(BF16)


def _fox_sample_kernel(q_ref, kc_ref, vc_ref, kn_ref, vn_ref, ckc_ref, ckn_ref, o_ref, *, nh, past):
    ts = q_ref.shape[0]
    row = lax.broadcasted_iota(jnp.int32, (ts, ts), 0)
    col = lax.broadcasted_iota(jnp.int32, (ts, ts), 1)
    for h in range(nh):
        sl = slice(h * HEAD_DIM, (h + 1) * HEAD_DIM)
        q = q_ref[:, sl]
        s1 = _dot_nt(q, _head_rows(kc_ref, h, nh, past)) - ckc_ref[h:h + 1, :] * LOG2E
        s2 = _dot_nt(q, kn_ref[:, sl]) - ckn_ref[h:h + 1, :] * LOG2E
        s2 = jnp.where(col <= row, s2, NEG_INF)
        m = jnp.maximum(jnp.max(s1, axis=-1, keepdims=True), jnp.max(s2, axis=-1, keepdims=True))
        p1 = jnp.exp2(s1 - m)
        p2 = jnp.exp2(s2 - m)
        l = jnp.sum(p1, axis=-1, keepdims=True) + jnp.sum(p2, axis=-1, keepdims=True)
        acc = _dot(p1.astype(BF16), _head_rows(vc_ref, h, nh, past)) + _dot(p2.astype(BF16), vn_ref[:, sl])
        o_ref[:, sl] = acc / l


def _sb_sample_kernel(q_ref, kc_ref, vc_ref, kn_ref, vn_ref, tric_ref, trin_ref, o_ref, *, nh, past, tc):
    ts = q_ref.shape[0]
    row = lax.broadcasted_iota(jnp.int32, (ts, ts), 0)
    col = lax.broadcasted_iota(jnp.int32, (ts, ts), 1)
    valid = col < row
    tric = tric_ref[...]
    trin = trin_ref[...]
    nchunk = past // tc
    for h in range(nh):
        sl = slice(h * HEAD_DIM, (h + 1) * HEAD_DIM)
        q = q_ref[:, sl]
        vc = _head_rows(vc_ref, h, nh, past)
        z2 = _dot_nt(q, kn_ref[:, sl])
        incl2 = _dot(_split2(jnp.where(valid, _softplus2(z2), 0.0)), trin)
        a2 = jnp.where(valid, jnp.exp2(z2 - incl2), 0.0)
        acc = _dot(a2.astype(BF16), vn_ref[:, sl])
        carry = incl2[:, :1]
        z1 = _dot_nt(q, _head_rows(kc_ref, h, nh, past))
        sp1 = _softplus2(z1)
        stacked = jnp.concatenate([sp1[:, c * tc:(c + 1) * tc] for c in range(nchunk)], axis=0)
        incl = _dot(_split2(stacked), tric)
        for c in reversed(range(nchunk)):
            inc_c = incl[c * ts:(c + 1) * ts, :]
            a = jnp.exp2(z1[:, c * tc:(c + 1) * tc] - inc_c - carry)
            acc = acc + _dot(a.astype(BF16), vc[c * tc:(c + 1) * tc, :])
            carry = carry + inc_c[:, :1]
        o_ref[:, sl] = acc


def _sample_specs(ts, past, nh):
    new = pl.BlockSpec((ts, nh * HEAD_DIM), lambda i: (i, 0))
    cache = pl.BlockSpec((None, past * nh, HEAD_DIM), lambda i: (i, 0, 0))
    return new, cache


def _fox_sample(q, kc, vc, kn, vn, ckc, ckn, b, ts, past, h):
    new, cache = _sample_specs(ts, past, h)
    return pl.pallas_call(
        functools.partial(_fox_sample_kernel, nh=h, past=past),
        grid=(b,),
        in_specs=[new, cache, cache, new, new,
                  pl.BlockSpec((None, h, past), lambda i: (i, 0, 0)),
                  pl.BlockSpec((None, h, ts), lambda i: (i, 0, 0))],
        out_specs=new,
        out_shape=jax.ShapeDtypeStruct((b * ts, h * HEAD_DIM), F32),
        compiler_params=_params("parallel"),
        name="fox_sample",
    )(q, kc, vc, kn, vn, ckc, ckn)


def _sb_sample(q, kc, vc, kn, vn, b, ts, past, h):
    new, cache = _sample_specs(ts, past, h)
    tc = _pick(past, (256, 128))
    return pl.pallas_call(
        functools.partial(_sb_sample_kernel, nh=h, past=past, tc=tc),
        grid=(b,),
        in_specs=[new, cache, cache, new, new,
                  pl.BlockSpec((2 * tc, tc), lambda i: (0, 0)),
                  pl.BlockSpec((2 * ts, ts), lambda i: (0, 0))],
        out_specs=new,
        out_shape=jax.ShapeDtypeStruct((b * ts, h * HEAD_DIM), F32),
        compiler_params=_params("parallel"),
        name="sb_sample",
    )(q, kc, vc, kn, vn, _tri2(tc), _tri2(ts))


def _layer(x, b, t, w, caches):
    hf = w["b_forget"].shape[0]
    hs = w["w_sb"][0].shape[0] // HEAD_DIM
    q_scale = HEAD_DIM ** -0.5 * LOG2E
    big = (1024, 512, 256, 128)

    hn = _rmsnorm(x, w["attn_norm"], BF16)
    qkv = {}
    for grp in ("fox", "sb"):
        wq, wk, wv = w["w_" + grp]
        (q,) = _matmul_call(functools.partial(_proj_q_kernel, scale=q_scale), hn, [wq], [], [], [BF16], big,
                            "proj_q_" + grp, weights_t=True)
        k32, k16 = _proj_kv(hn, wk, "proj_k_" + grp)
        v32, v16 = _proj_kv(hn, wv, "proj_v_" + grp)
        qkv[grp] = (q, k32, k16, v32, v16)
    (logf_pad,) = _matmul_call(_proj_logf_kernel, hn, [w["w_fl"]], [w["b_fl"]], [], [F32], (LANES,), "proj_logf",
                               weights_t=True)

    qf, kf32, kf, vf32, vf = qkv["fox"]
    qs, ks32, ks, vs32, vs = qkv["sb"]
    zeros_init = jnp.zeros((b, 1, LANES), F32)
    if caches is None:
        ct, _ = _cumsum_t(logf_pad.reshape(b, t, LANES), zeros_init)
        o_fox, (w["w_gate"], w["w_up"]) = _fox_prompt(qf, kf, vf, ct[:, :hf, :].reshape(b, hf, 1, t),
                                                       [w["w_gate_f32"], w["w_up_f32"]], b, t, hf)
        o_sb, (w["w_down"], w["w_out"]) = _sb_prompt(qs, ks, vs, [w["w_down_f32"], w["w_out_f32"]], b, t, hs)
    else:
        c_fk, c_fv, c_fl, c_sk, c_sv = caches
        past = c_fk.shape[1]
        c_fl_pad = jnp.pad(c_fl.astype(F32), ((0, 0), (0, 0), (0, LANES - hf)))
        ct_c, last = _cumsum_t(c_fl_pad, zeros_init)
        ct_n, _ = _cumsum_t(logf_pad.reshape(b, t, LANES), last)
        o_fox = _fox_sample(qf, c_fk.reshape(b, past * hf, HEAD_DIM), c_fv.reshape(b, past * hf, HEAD_DIM), kf, vf,
                            ct_c[:, :hf, :], ct_n[:, :hf, :], b, t, past, hf)
        o_sb = _sb_sample(qs, c_sk.reshape(b, past * hs, HEAD_DIM), c_sv.reshape(b, past * hs, HEAD_DIM), ks, vs,
                          b, t, past, hs)

    a = _merge_norm(o_fox, o_sb, w["out_norm_fox"], w["out_norm_sb"])
    (x1,) = _matmul_call(_proj_resid_kernel, a, [w["w_out"]], [], [x], [F32], big, "proj_out")
    h2 = _rmsnorm(x1, w["ffn_norm"], BF16)
    (act,) = _matmul_call(_proj_swiglu_kernel, h2, [w["w_gate"], w["w_up"]], [], [], [BF16], (256, 128), "ffn_up",
                          tm_candidates=(2048, 1024, 512, 256, 128, 64, 32, 16, 8))
    (x2,) = _matmul_call(_proj_resid_kernel, act, [w["w_down"]], [], [x1], [F32], (512, 256, 128), "ffn_down",
                         tm_candidates=(512, 256, 128, 64, 32, 16, 8))
    return x2, kf32, vf32, logf_pad[:, :hf], ks32, vs32


def kernel(x_prompt, x_sample, cache_fox_k, cache_fox_v, cache_fox_logf, cache_sb_k, cache_sb_v, attn_norm, w_in,
           b_forget, out_norm_fox, out_norm_sb, w_out, ffn_norm, w_gate, w_up, w_down, final_norm):
    bp, tp, d = x_prompt.shape
    bs, ts, _ = x_sample.shape
    depth = w_in.shape[0]
    hf = cache_fox_k.shape[3]
    hs = cache_sb_k.shape[3]
    w_fox = hf * HEAD_DIM
    w_sb = hs * HEAD_DIM

    xp = x_prompt.reshape(bp * tp, d)
    xs = x_sample.reshape(bs * ts, d)
    outs_p, outs_s = [], []
    for l in range(depth):
        cuts = [0, w_fox, 2 * w_fox, 3 * w_fox, 3 * w_fox + hf, 3 * w_fox + hf + w_sb, 3 * w_fox + hf + 2 * w_sb,
                3 * w_fox + hf + 3 * w_sb]
        wl = jnp.swapaxes(w_in[l], 0, 1)
        seg = [wl[cuts[i]:cuts[i + 1], :] for i in range(7)]
        w = {
            "attn_norm": attn_norm[l], "ffn_norm": ffn_norm[l],
            "out_norm_fox": out_norm_fox[l], "out_norm_sb": out_norm_sb[l],
            "w_fox": [s.astype(BF16) for s in seg[0:3]],
            "w_sb": [s.astype(BF16) for s in seg[4:7]],
            "w_fl": jnp.pad(seg[3], ((0, LANES - hf), (0, 0))).astype(BF16),
            "b_fl": jnp.pad(b_forget[l].astype(F32), (0, LANES - hf)).reshape(1, LANES),
            "b_forget": b_forget[l],
            "w_out_f32": w_out[l], "w_gate_f32": w_gate[l], "w_up_f32": w_up[l], "w_down_f32": w_down[l],
        }
        xp, *rest_p = _layer(xp, bp, tp, w, None)
        caches = (cache_fox_k[l], cache_fox_v[l], cache_fox_logf[l], cache_sb_k[l], cache_sb_v[l])
        xs, *rest_s = _layer(xs, bs, ts, w, caches)
        outs_p.append(rest_p)
        outs_s.append(rest_s)

    y_prompt = _rmsnorm(xp, final_norm, F32).reshape(bp, tp, d)
    y_sample = _rmsnorm(xs, final_norm, F32).reshape(bs, ts, d)

    def stack(outs, idx, shape):
        return jnp.stack([o[idx].reshape(shape) for o in outs])

    res = [y_prompt, y_sample]
    for outs, b, t in ((outs_p, bp, tp), (outs_s, bs, ts)):
        res += [stack(outs, 0, (b, t, hf, HEAD_DIM)), stack(outs, 1, (b, t, hf, HEAD_DIM)),
                stack(outs, 2, (b, t, hf)),
                stack(outs, 3, (b, t, hs, HEAD_DIM)), stack(outs, 4, (b, t, hs, HEAD_DIM))]
    return tuple(res)
```

```python
import functools
import math

import jax
import jax.numpy as jnp
from jax import lax
from jax.experimental import pallas as pl
from jax.experimental.pallas import tpu as pltpu

EPS = 1e-6
NEG_INF = -1e30
HEAD_DIM = 128
LANES = 128
SUBLANES = 8
LOG2E = math.log2(math.e)
VMEM_LIMIT_BYTES = 56 * 1024 * 1024
SCORE_BUFFERS = 3

F32 = jnp.float32
BF16 = jnp.bfloat16


def _params(*sem):
    return pltpu.CompilerParams(dimension_semantics=sem, vmem_limit_bytes=VMEM_LIMIT_BYTES)


def _pick(n, candidates):
    for c in candidates:
        if n % c == 0:
            return c
    return n


def _softplus(z):
    return jnp.maximum(z, 0.0) + jnp.log1p(jnp.exp(-jnp.abs(z)))


def _softplus2(z):
    return jnp.maximum(z, 0.0) + jnp.log2(1.0 + jnp.exp2(-jnp.abs(z)))


def _dot(a, b):
    return jnp.dot(a, b, preferred_element_type=F32)


def _dot_nt(a, b):
    return lax.dot_general(a, b, (((1,), (1,)), ((), ())), preferred_element_type=F32)


def _split2(x):
    hi = x.astype(BF16)
    lo = (x - hi.astype(F32)).astype(BF16)
    return jnp.concatenate([hi, lo], axis=1)


def _split3(x):
    hi = x.astype(BF16)
    r = x - hi.astype(F32)
    mid = r.astype(BF16)
    lo = (r - mid.astype(F32)).astype(BF16)
    return hi, mid, lo


def _rmsnorm_kernel(x_ref, g_ref, o_ref):
    x = x_ref[...]
    r = lax.rsqrt(jnp.mean(x * x, axis=-1, keepdims=True) + EPS)
    o_ref[...] = (x * r * g_ref[...]).astype(o_ref.dtype)


def _rmsnorm(x, g, out_dtype):
    m, d = x.shape
    tm = _pick(m, (256, 128, 64, 32, 16, 8))
    return pl.pallas_call(
        _rmsnorm_kernel,
        grid=(m // tm,),
        in_specs=[pl.BlockSpec((tm, d), lambda i: (i, 0)),
                  pl.BlockSpec((1, d), lambda i: (0, 0))],
        out_specs=pl.BlockSpec((tm, d), lambda i: (i, 0)),
        out_shape=jax.ShapeDtypeStruct((m, d), out_dtype),
        compiler_params=_params("parallel"),
        name="rmsnorm",
    )(x, g.reshape(1, d).astype(F32))


def _merge_norm_kernel(a_ref, b_ref, ga_ref, gb_ref, o_ref):
    wa = a_ref.shape[1]
    for ref, g_ref, lo in ((a_ref, ga_ref, 0), (b_ref, gb_ref, wa)):
        x = ref[...]
        r = lax.rsqrt(jnp.mean(x * x, axis=-1, keepdims=True) + EPS)
        o_ref[:, lo:lo + x.shape[1]] = (x * r * g_ref[...]).astype(o_ref.dtype)


def _merge_norm(o_fox, o_sb, g_fox, g_sb):
    m, wa = o_fox.shape
    wb = o_sb.shape[1]
    tm = _pick(m, (256, 128, 64, 32, 16, 8))
    return pl.pallas_call(
        _merge_norm_kernel,
        grid=(m // tm,),
        in_specs=[pl.BlockSpec((tm, wa), lambda i: (i, 0)),
                  pl.BlockSpec((tm, wb), lambda i: (i, 0)),
                  pl.BlockSpec((1, wa), lambda i: (0, 0)),
                  pl.BlockSpec((1, wb), lambda i: (0, 0))],
        out_specs=pl.BlockSpec((tm, wa + wb), lambda i: (i, 0)),
        out_shape=jax.ShapeDtypeStruct((m, wa + wb), BF16),
        compiler_params=_params("parallel"),
        name="merge_norm",
    )(o_fox, o_sb, g_fox.reshape(1, wa).astype(F32), g_sb.reshape(1, wb).astype(F32))


def _proj_q_kernel(x_ref, w_ref, *rest, scale, n_side):
    side_in, (o_ref, *side_out) = rest[:n_side], rest[n_side:]
    _side_cast(list(side_in) + side_out)
    o_ref[...] = (_dot_nt(x_ref[...], w_ref[...]) * scale).astype(o_ref.dtype)


def _proj_q(x, w, scale, side, name):
    m, k = x.shape
    n = w.shape[0]
    tm = _pick(m, (1024, 512, 256, 128, 64, 32, 16, 8))
    tn = _pick(n, (1024, 512, 256, 128))
    nj = n // tn
    steps = (m // tm) * nj
    aligned = [row0 % SUBLANES == 0 for _, row0, _ in side]
    plain = [arr[row0:row0 + count].astype(BF16) for (arr, row0, count), ok in zip(side, aligned) if not ok]
    kept = [s for s, ok in zip(side, aligned) if ok]
    side_in, side_out, side_shapes = [], [], []
    for arr, row0, count in kept:
        rows = _side_rows(count, steps)
        last = count // rows - 1
        block = lambda i, j, last=last: jnp.minimum(i * nj + j, last)
        side_in.append(pl.BlockSpec((pl.Element(rows), pl.Element(arr.shape[1])),
                                    lambda i, j, row0=row0, rows=rows, block=block:
                                    (SUBLANES * (row0 // SUBLANES + block(i, j) * (rows // SUBLANES)), 0)))
        side_out.append(pl.BlockSpec((rows, arr.shape[1]), lambda i, j, block=block: (block(i, j), 0)))
        side_shapes.append(jax.ShapeDtypeStruct((count, arr.shape[1]), BF16))
    q, *cast = pl.pallas_call(
        functools.partial(_proj_q_kernel, scale=scale, n_side=len(kept)),
        grid=(m // tm, nj),
        in_specs=[pl.BlockSpec((tm, k), lambda i, j: (i, 0)), pl.BlockSpec((tn, k), lambda i, j: (j, 0))] + side_in,
        out_specs=[pl.BlockSpec((tm, tn), lambda i, j: (i, j))] + side_out,
        out_shape=[jax.ShapeDtypeStruct((m, n), BF16)] + side_shapes,
        compiler_params=_params("arbitrary", "arbitrary"),
        name=name,
    )(x, w, *[arr for arr, _, _ in kept])
    cast, plain = iter(cast), iter(plain)
    return q, [next(cast) if ok else next(plain) for ok in aligned]


def _proj_kv_kernel(x_ref, w_ref, o32_ref, o16_ref, *, nh):
    acc = _dot_nt(x_ref[...], w_ref[...])
    o16_ref[...] = acc.astype(o16_ref.dtype)
    tm = x_ref.shape[0]
    for h in range(nh):
        o32_ref[pl.ds(h, tm, stride=nh), :] = acc[:, h * HEAD_DIM:(h + 1) * HEAD_DIM]


def _proj_kv(x, w, name):
    m, k = x.shape
    n = w.shape[0]
    nh = n // HEAD_DIM
    tm = _pick(m, (512, 256, 128, 64, 32, 16, 8))
    return pl.pallas_call(
        functools.partial(_proj_kv_kernel, nh=nh),
        grid=(m // tm,),
        in_specs=[pl.BlockSpec((tm, k), lambda i: (i, 0)),
                  pl.BlockSpec((n, k), lambda i: (0, 0))],
        out_specs=[pl.BlockSpec((tm * nh, HEAD_DIM), lambda i: (i, 0)),
                   pl.BlockSpec((tm, n), lambda i: (i, 0))],
        out_shape=[jax.ShapeDtypeStruct((m * nh, HEAD_DIM), F32), jax.ShapeDtypeStruct((m, n), BF16)],
        compiler_params=_params("parallel"),
        name=name,
    )(x, w)


def _proj_logf_kernel(x_ref, w_ref, b_ref, o_ref):
    o_ref[...] = -_softplus(-(_dot_nt(x_ref[...], w_ref[...]) + b_ref[...]))


def _proj_resid_kernel(x_ref, w_ref, r_ref, o_ref):
    o_ref[...] = r_ref[...] + _dot(x_ref[...], w_ref[...])


def _proj_swiglu_kernel(x_ref, wg_ref, wu_ref, o_ref):
    x = x_ref[...]
    g = _dot(x, wg_ref[...])
    u = _dot(x, wu_ref[...])
    o_ref[...] = (g * jax.nn.sigmoid(g) * u).astype(o_ref.dtype)


def _matmul_call(kernel, x, weights, extra_row_inputs, extra_tile_inputs, out_dtypes, tn_candidates, name,
                 tm_candidates=(1024, 512, 256, 128, 64, 32, 16, 8), weights_t=False):
    m, k = x.shape
    n = weights[0].shape[0 if weights_t else 1]
    tm = _pick(m, tm_candidates)
    tn = _pick(n, tn_candidates)
    tile = pl.BlockSpec((tm, tn), lambda i, j: (i, j))
    in_specs = [pl.BlockSpec((tm, k), lambda i, j: (i, 0))]
    w_spec = pl.BlockSpec((tn, k), lambda i, j: (j, 0)) if weights_t else pl.BlockSpec((k, tn), lambda i, j: (0, j))
    in_specs += [w_spec for _ in weights]
    in_specs += [pl.BlockSpec((1, tn), lambda i, j: (0, j)) for _ in extra_row_inputs]
    in_specs += [tile for _ in extra_tile_inputs]
    outs = pl.pallas_call(
        kernel,
        grid=(m // tm, n // tn),
        in_specs=in_specs,
        out_specs=[tile for _ in out_dtypes],
        out_shape=[jax.ShapeDtypeStruct((m, n), dt) for dt in out_dtypes],
        compiler_params=_params("parallel", "arbitrary"),
        name=name,
    )(x, *weights, *extra_row_inputs, *extra_tile_inputs)
    return outs


def _cumsum_kernel(x_ref, init_ref, tril_ref, ct_ref, last_ref, carry_sc):
    t = pl.program_id(1)

    @pl.when(t == 0)
    def _():
        carry_sc[...] = init_ref[...]

    parts = jnp.concatenate(_split3(x_ref[...]), axis=1)
    s = _dot(tril_ref[...], parts)
    cum = s[:, :LANES] + s[:, LANES:2 * LANES] + s[:, 2 * LANES:] + carry_sc[...]
    carry_sc[...] = cum[-1:, :]
    ct_ref[...] = cum.T
    last_ref[...] = cum[-1:, :]


def _cumsum_t(x, init):
    b, t, _ = x.shape
    tt = _pick(t, (512, 256, 128))
    idx = jnp.arange(tt)
    tril = (idx[:, None] >= idx[None, :]).astype(BF16)
    return pl.pallas_call(
        _cumsum_kernel,
        grid=(b, t // tt),
        in_specs=[pl.BlockSpec((None, tt, LANES), lambda i, j: (i, j, 0)),
                  pl.BlockSpec((None, 1, LANES), lambda i, j: (i, 0, 0)),
                  pl.BlockSpec((tt, tt), lambda i, j: (0, 0))],
        out_specs=[pl.BlockSpec((None, LANES, tt), lambda i, j: (i, 0, j)),
                   pl.BlockSpec((None, 1, LANES), lambda i, j: (i, 0, 0))],
        out_shape=[jax.ShapeDtypeStruct((b, LANES, t), F32),
                   jax.ShapeDtypeStruct((b, 1, LANES), F32)],
        scratch_shapes=[pltpu.VMEM((1, LANES), F32)],
        compiler_params=_params("parallel", "arbitrary"),
        name="cumsum_t",
    )(x, init, tril)


def _side_rows(k, steps):
    for rows in range(16, k + 1, 16):
        if k % rows == 0 and k // rows <= steps:
            return rows
    return None


def _side_specs(weights, steps, step_index):
    specs, shapes = [], []
    for w in weights:
        rows = _side_rows(w.shape[0], steps)
        last = w.shape[0] // rows - 1
        specs.append(pl.BlockSpec((rows, w.shape[1]),
                                  lambda *g, last=last: (jnp.minimum(step_index(*g), last), 0)))
        shapes.append(jax.ShapeDtypeStruct(w.shape, BF16))
    return specs, shapes


def _side_cast(refs):
    n = len(refs) // 2
    for w_ref, o_ref in zip(refs[:n], refs[n:]):
        o_ref[...] = w_ref[...].astype(o_ref.dtype)


def _fox_prompt_kernel(q_ref, k_ref, v_ref, ck_ref, *rest, tq, tk, n_side):
    side_in, (o_ref, *side_out) = rest[:n_side], rest[n_side:2 * n_side + 1]
    kaug_sc, vaug_sc, m_sc, acc_sc, s_sc = rest[2 * n_side + 1:]
    _side_cast(list(side_in) + side_out)
    t = k_ref.shape[0]
    nq = t // tq
    nbuf = s_sc.shape[0]
    hi, mid, lo = _split3(ck_ref[...] * LOG2E)
    rows = lax.broadcasted_iota(jnp.int32, (HEAD_DIM, t), 0)
    parts = jnp.where(rows == 0, hi.astype(F32),
                      jnp.where(rows == 1, mid.astype(F32), jnp.where(rows == 2, lo.astype(F32), 0.0)))
    kaug_sc[:, :HEAD_DIM] = k_ref[...]
    kaug_sc[:, HEAD_DIM:] = parts.T.astype(BF16)
    lane = lax.broadcasted_iota(jnp.int32, (t, HEAD_DIM), 1)
    vaug_sc[:, :HEAD_DIM] = v_ref[...]
    vaug_sc[:, HEAD_DIM:] = jnp.where(lane == 0, 1.0, 0.0).astype(BF16)

    lane_q = lax.broadcasted_iota(jnp.int32, (tq, HEAD_DIM), 1)
    q_extra = jnp.where(lane_q < 3, -1.0, 0.0).astype(BF16)
    nc = tk // LANES
    blocks = [(qi, kb) for qi in range(nq) for kb in range((qi * tq) // tk + 1)]

    def scores(n):
        qi, kb = blocks[n]
        qa = jnp.concatenate([q_ref[qi * tq:(qi + 1) * tq, :], q_extra], axis=1)
        s_sc[n % nbuf] = _dot_nt(qa, kaug_sc[kb * tk:(kb + 1) * tk, :])

    def softmax_pv(n):
        qi, kb = blocks[n]
        first = kb == 0
        last = kb == (qi * tq) // tk
        s = s_sc[n % nbuf]
        if last:
            row = qi * tq + lax.broadcasted_iota(jnp.int32, s.shape, 0)
            col = kb * tk + lax.broadcasted_iota(jnp.int32, s.shape, 1)
            s = jnp.where(col <= row, s, NEG_INF)
        chunks = [s[:, c * LANES:(c + 1) * LANES] for c in range(nc)]
        bm = jnp.max(functools.reduce(jnp.maximum, chunks), axis=-1, keepdims=True)
        if first:
            m_new = jnp.broadcast_to(bm, (tq, LANES))
        else:
            m_prev = m_sc[qi]
            m_new = jnp.maximum(m_prev, bm)
            alpha = jnp.exp2(m_prev - m_new)
        p = jnp.concatenate([jnp.exp2(c - m_new).astype(BF16) for c in chunks], axis=1)
        pv = _dot(p, vaug_sc[kb * tk:(kb + 1) * tk, :])
        acc = pv if first else jnp.concatenate([alpha, alpha], axis=1) * acc_sc[qi] + pv
        if last:
            o_ref[qi * tq:(qi + 1) * tq, :] = acc[:, :HEAD_DIM] / acc[:, HEAD_DIM:HEAD_DIM + 1]
        else:
            acc_sc[qi] = acc
            m_sc[qi] = m_new

    scores(0)
    for n in range(len(blocks)):
        if n + 1 < len(blocks):
            scores(n + 1)
        softmax_pv(n)


def _fox_prompt(q, k, v, ck, side, b, t, h):
    tq = _pick(t, (512, 256, 128))
    nq = t // tq
    blk = pl.BlockSpec((t, HEAD_DIM), lambda i, j: (i, j))
    side_specs, side_shapes = _side_specs(side, b * h, lambda i, j: i * h + j)
    o, *cast = pl.pallas_call(
        functools.partial(_fox_prompt_kernel, tq=tq, tk=tq, n_side=len(side)),
        grid=(b, h),
        in_specs=[blk, blk, blk, pl.BlockSpec((None, None, 1, t), lambda i, j: (i, j, 0, 0))] + side_specs,
        out_specs=[blk] + side_specs,
        out_shape=[jax.ShapeDtypeStruct((b * t, h * HEAD_DIM), F32)] + side_shapes,
        scratch_shapes=[pltpu.VMEM((t, 2 * HEAD_DIM), BF16), pltpu.VMEM((t, 2 * HEAD_DIM), BF16),
                        pltpu.VMEM((nq, tq, LANES), F32), pltpu.VMEM((nq, tq, 2 * HEAD_DIM), F32),
                        pltpu.VMEM((SCORE_BUFFERS, tq, tq), F32)],
        compiler_params=_params("arbitrary", "arbitrary"),
        name="fox_prompt",
    )(q, k, v, ck, *side)
    return o, cast


def _tri2(tk):
    idx = jnp.arange(tk)
    tri = (idx[:, None] >= idx[None, :]).astype(BF16)
    return jnp.concatenate([tri, tri], axis=0)


SKIP_LOG2 = 152.0


def _sb_prompt_kernel(q_ref, k_ref, v_ref, tri2_ref, *rest, tb, n_side):
    side_in, (o_ref, *side_out) = rest[:n_side], rest[n_side:2 * n_side + 1]
    carry_sc, acc_sc, z_sc = rest[2 * n_side + 1:]
    _side_cast(list(side_in) + side_out)
    t = k_ref.shape[0]
    nq = t // tb
    nbuf = z_sc.shape[0]
    tri2 = tri2_ref[...]
    nc = tb // LANES

    def q_tile(qi):
        return q_ref[qi * tb:(qi + 1) * tb, :]

    def weights(z, carry, diagonal):
        sp = _softplus2(z)
        if diagonal:
            row = lax.broadcasted_iota(jnp.int32, z.shape, 0)
            col = lax.broadcasted_iota(jnp.int32, z.shape, 1)
            valid = col < row
            sp = jnp.where(valid, sp, 0.0)
        incl = _dot(_split2(sp), tri2)
        arg = z - incl
        if carry is not None:
            arg = arg - jnp.concatenate([carry] * nc, axis=1)
        a = jnp.exp2(arg)
        if diagonal:
            a = jnp.where(valid, a, 0.0)
        tot = incl[:, :1]
        new_carry = jnp.broadcast_to(tot, (tb, LANES)) if carry is None else carry + tot
        return a.astype(BF16), new_carry

    blocks = [(qi, kb) for qi in range(nq) for kb in (qi, qi - 1) if kb >= 0]

    def scores(n):
        qi, kb = blocks[n]
        z_sc[n % nbuf] = _dot_nt(q_tile(qi), k_ref[kb * tb:(kb + 1) * tb, :])

    def near(n):
        qi, kb = blocks[n]
        first = kb == qi
        a, carry = weights(z_sc[n % nbuf], None if first else carry_sc[qi], first)
        pv = _dot(a, v_ref[kb * tb:(kb + 1) * tb, :])
        acc = pv if first else acc_sc[qi] + pv
        if kb == qi - 1 or kb == 0:
            o_ref[qi * tb:(qi + 1) * tb, :] = acc
        if kb > 0:
            acc_sc[qi] = acc
            carry_sc[qi] = carry

    scores(0)
    for n in range(len(blocks)):
        if n + 1 < len(blocks):
            scores(n + 1)
        near(n)

    if nq <= 2:
        return
    lightest = functools.reduce(jnp.minimum, [carry_sc[qi] for qi in range(2, nq)])

    @pl.when(jnp.min(lightest) <= SKIP_LOG2)
    def _():
        for qi in range(2, nq):
            def more(kb):
                return jnp.logical_and(kb >= 0, jnp.min(carry_sc[qi]) <= SKIP_LOG2)

            def far(kb):
                start = pl.multiple_of(kb * tb, tb)
                z = _dot_nt(q_tile(qi), k_ref[pl.ds(start, tb), :])
                a, carry = weights(z, carry_sc[qi], False)
                acc_sc[qi] = acc_sc[qi] + _dot(a, v_ref[pl.ds(start, tb), :])
                carry_sc[qi] = carry
                return kb - 1

            lax.while_loop(more, far, qi - 2)
            o_ref[qi * tb:(qi + 1) * tb, :] = acc_sc[qi]


def _sb_prompt(q, k, v, side, b, t, h):
    tb = _pick(t, (256, 128))
    nq = t // tb
    blk = pl.BlockSpec((t, HEAD_DIM), lambda i, j: (i, j))
    side_specs, side_shapes = _side_specs(side, b * h, lambda i, j: i * h + j)
    o, *cast = pl.pallas_call(
        functools.partial(_sb_prompt_kernel, tb=tb, n_side=len(side)),
        grid=(b, h),
        in_specs=[blk, blk, blk, pl.BlockSpec((2 * tb, tb), lambda i, j: (0, 0))] + side_specs,
        out_specs=[blk] + side_specs,
        out_shape=[jax.ShapeDtypeStruct((b * t, h * HEAD_DIM), F32)] + side_shapes,
        scratch_shapes=[pltpu.VMEM((nq, tb, LANES), F32), pltpu.VMEM((nq, tb, HEAD_DIM), F32),
                        pltpu.VMEM((SCORE_BUFFERS, tb, tb), F32)],
        compiler_params=_params("arbitrary", "arbitrary"),
        name="sb_prompt",
    )(q, k, v, _tri2(tb), *side)
    return o, cast


def _head_rows(ref, h, nh, past):
    return ref[pl.ds(h, past, stride=nh), :].astype(BF16)


def _fox_sample_kernel(q_ref, kc_ref, vc_ref, kn_ref, vn_ref, ckc_ref, ckn_ref, o_ref, *, nh, past):
    ts = q_ref.shape[0]
    row = lax.broadcasted_iota(jnp.int32, (ts, ts), 0)
    col = lax.broadcasted_iota(jnp.int32, (ts, ts), 1)
    for h in range(nh):
        sl = slice(h * HEAD_DIM, (h + 1) * HEAD_DIM)
        q = q_ref[:, sl]
        s1 = _dot_nt(q, _head_rows(kc_ref, h, nh, past)) - ckc_ref[h:h + 1, :] * LOG2E
        s2 = _dot_nt(q, kn_ref[:, sl]) - ckn_ref[h:h + 1, :] * LOG2E
        s2 = jnp.where(col <= row, s2, NEG_INF)
        m = jnp.maximum(jnp.max(s1, axis=-1, keepdims=True), jnp.max(s2, axis=-1, keepdims=True))
        p1 = jnp.exp2(s1 - m)
        p2 = jnp.exp2(s2 - m)
        l = jnp.sum(p1, axis=-1, keepdims=True) + jnp.sum(p2, axis=-1, keepdims=True)
        acc = _dot(p1.astype(BF16), _head_rows(vc_ref, h, nh, past)) + _dot(p2.astype(BF16), vn_ref[:, sl])
        o_ref[:, sl] = acc / l


def _sb_sample_kernel(q_ref, kc_ref, vc_ref, kn_ref, vn_ref, tric_ref, trin_ref, o_ref, *, nh, past, tc):
    ts = q_ref.shape[0]
    row = lax.broadcasted_iota(jnp.int32, (ts, ts), 0)
    col = lax.broadcasted_iota(jnp.int32, (ts, ts), 1)
    valid = col < row
    tric = tric_ref[...]
    trin = trin_ref[...]
    nchunk = past // tc
    for h in range(nh):
        sl = slice(h * HEAD_DIM, (h + 1) * HEAD_DIM)
        q = q_ref[:, sl]
        vc = _head_rows(vc_ref, h, nh, past)
        z2 = _dot_nt(q, kn_ref[:, sl])
        incl2 = _dot(_split2(jnp.where(valid, _softplus2(z2), 0.0)), trin)
        a2 = jnp.where(valid, jnp.exp2(z2 - incl2), 0.0)
        acc = _dot(a2.astype(BF16), vn_ref[:, sl])
        carry = incl2[:, :1]
        z1 = _dot_nt(q, _head_rows(kc_ref, h, nh, past))
        sp1 = _softplus2(z1)
        stacked = jnp.concatenate([sp1[:, c * tc:(c + 1) * tc] for c in range(nchunk)], axis=0)
        incl = _dot(_split2(stacked), tric)
        for c in reversed(range(nchunk)):
            inc_c = incl[c * ts:(c + 1) * ts, :]
            a = jnp.exp2(z1[:, c * tc:(c + 1) * tc] - inc_c - carry)
            acc = acc + _dot(a.astype(BF16), vc[c * tc:(c + 1) * tc, :])
            carry = carry + inc_c[:, :1]
        o_ref[:, sl] = acc


def _sample_specs(ts, past, nh):
    new = pl.BlockSpec((ts, nh * HEAD_DIM), lambda i: (i, 0))
    cache = pl.BlockSpec((None, past * nh, HEAD_DIM), lambda i: (i, 0, 0))
    return new, cache


def _fox_sample(q, kc, vc, kn, vn, ckc, ckn, b, ts, past, h):
    new, cache = _sample_specs(ts, past, h)
    return pl.pallas_call(
        functools.partial(_fox_sample_kernel, nh=h, past=past),
        grid=(b,),
        in_specs=[new, cache, cache, new, new,
                  pl.BlockSpec((None, h, past), lambda i: (i, 0, 0)),
                  pl.BlockSpec((None, h, ts), lambda i: (i, 0, 0))],
        out_specs=new,
        out_shape=jax.ShapeDtypeStruct((b * ts, h * HEAD_DIM), F32),
        compiler_params=_params("parallel"),
        name="fox_sample",
    )(q, kc, vc, kn, vn, ckc, ckn)


def _sb_sample(q, kc, vc, kn, vn, b, ts, past, h):
    new, cache = _sample_specs(ts, past, h)
    tc = _pick(past, (256, 128))
    return pl.pallas_call(
        functools.partial(_sb_sample_kernel, nh=h, past=past, tc=tc),
        grid=(b,),
        in_specs=[new, cache, cache, new, new,
                  pl.BlockSpec((2 * tc, tc), lambda i: (0, 0)),
                  pl.BlockSpec((2 * ts, ts), lambda i: (0, 0))],
        out_specs=new,
        out_shape=jax.ShapeDtypeStruct((b * ts, h * HEAD_DIM), F32),
        compiler_params=_params("parallel"),
        name="sb_sample",
    )(q, kc, vc, kn, vn, _tri2(tc), _tri2(ts))


def _layer(x, b, t, w, caches):
    hf = w["b_forget"].shape[0]
    hs = w["h_sb"]
    q_scale = HEAD_DIM ** -0.5 * LOG2E
    big = (1024, 512, 256, 128)

    hn = _rmsnorm(x, w["attn_norm"], BF16)
    qkv = {}
    for grp in ("fox", "sb"):
        pending = w["to_cast_" + grp] if caches is None else []
        q, cast = _proj_q(hn, w["w_q_" + grp], q_scale,
                          [(w["w_in_t"], row0, count) for _, row0, count in pending], "proj_q_" + grp)
        for (key, _, _), arr in zip(pending, cast):
            w[key] = arr
        wk, wv = w["w_k_" + grp], w["w_v_" + grp]
        k32, k16 = _proj_kv(hn, wk, "proj_k_" + grp)
        v32, v16 = _proj_kv(hn, wv, "proj_v_" + grp)
        qkv[grp] = (q, k32, k16, v32, v16)
    (logf_pad,) = _matmul_call(_proj_logf_kernel, hn, [w["w_fl"]], [w["b_fl"]], [], [F32], (LANES,), "proj_logf",
                               weights_t=True)

    qf, kf32, kf, vf32, vf = qkv["fox"]
    qs, ks32, ks, vs32, vs = qkv["sb"]
    zeros_init = jnp.zeros((b, 1, LANES), F32)
    if caches is None:
        ct, _ = _cumsum_t(logf_pad.reshape(b, t, LANES), zeros_init)
        o_fox, (w["w_gate"], w["w_up"]) = _fox_prompt(qf, kf, vf, ct[:, :hf, :].reshape(b, hf, 1, t),
                                                       [w["w_gate_f32"], w["w_up_f32"]], b, t, hf)
        o_sb, (w["w_down"], w["w_out"]) = _sb_prompt(qs, ks, vs, [w["w_down_f32"], w["w_out_f32"]], b, t, hs)
    else:
        c_fk, c_fv, c_fl, c_sk, c_sv = caches
        past = c_fk.shape[1]
        c_fl_pad = jnp.pad(c_fl.astype(F32), ((0, 0), (0, 0), (0, LANES - hf)))
        ct_c, last = _cumsum_t(c_fl_pad, zeros_init)
        ct_n, _ = _cumsum_t(logf_pad.reshape(b, t, LANES), last)
        o_fox = _fox_sample(qf, c_fk.reshape(b, past * hf, HEAD_DIM), c_fv.reshape(b, past * hf, HEAD_DIM), kf, vf,
                            ct_c[:, :hf, :], ct_n[:, :hf, :], b, t, past, hf)
        o_sb = _sb_sample(qs, c_sk.reshape(b, past * hs, HEAD_DIM), c_sv.reshape(b, past * hs, HEAD_DIM), ks, vs,
                          b, t, past, hs)

    a = _merge_norm(o_fox, o_sb, w["out_norm_fox"], w["out_norm_sb"])
    (x1,) = _matmul_call(_proj_resid_kernel, a, [w["w_out"]], [], [x], [F32], big, "proj_out")
    h2 = _rmsnorm(x1, w["ffn_norm"], BF16)
    (act,) = _matmul_call(_proj_swiglu_kernel, h2, [w["w_gate"], w["w_up"]], [], [], [BF16], (256, 128), "ffn_up",
                          tm_candidates=(2048, 1024, 512, 256, 128, 64, 32, 16, 8))
    (x2,) = _matmul_call(_proj_resid_kernel, act, [w["w_down"]], [], [x1], [F32], (512, 256, 128), "ffn_down",
                         tm_candidates=(512, 256, 128, 64, 32, 16, 8))
    return x2, kf32, vf32, logf_pad[:, :hf], ks32, vs32


def kernel(x_prompt, x_sample, cache_fox_k, cache_fox_v, cache_fox_logf, cache_sb_k, cache_sb_v, attn_norm, w_in,
           b_forget, out_norm_fox, out_norm_sb, w_out, ffn_norm, w_gate, w_up, w_down, final_norm):
    bp, tp, d = x_prompt.shape
    bs, ts, _ = x_sample.shape
    depth = w_in.shape[0]
    hf = cache_fox_k.shape[3]
    hs = cache_sb_k.shape[3]
    w_fox = hf * HEAD_DIM
    w_sb = hs * HEAD_DIM

    xp = x_prompt.reshape(bp * tp, d)
    xs = x_sample.reshape(bs * ts, d)
    outs_p, outs_s = [], []
    for l in range(depth):
        cuts = [0, w_fox, 2 * w_fox, 3 * w_fox, 3 * w_fox + hf, 3 * w_fox + hf + w_sb, 3 * w_fox + hf + 2 * w_sb,
                3 * w_fox + hf + 3 * w_sb]
        wl = jnp.swapaxes(w_in[l], 0, 1)
        w = {
            "attn_norm": attn_norm[l], "ffn_norm": ffn_norm[l],
            "out_norm_fox": out_norm_fox[l], "out_norm_sb": out_norm_sb[l],
            "h_sb": hs, "w_in_t": wl,
            "w_q_fox": wl[cuts[0]:cuts[1], :].astype(BF16),
            "to_cast_fox": [("w_k_fox", cuts[1], w_fox), ("w_v_fox", cuts[2], w_fox), ("w_q_sb", cuts[4], w_sb)],
            "to_cast_sb": [("w_k_sb", cuts[5], w_sb), ("w_v_sb", cuts[6], w_sb)],
            "w_fl": jnp.pad(wl[cuts[3]:cuts[4], :], ((0, LANES - hf), (0, 0))).astype(BF16),
            "b_fl": jnp.pad(b_forget[l].astype(F32), (0, LANES - hf)).reshape(1, LANES),
            "b_forget": b_forget[l],
            "w_out_f32": w_out[l], "w_gate_f32": w_gate[l], "w_up_f32": w_up[l], "w_down_f32": w_down[l],
        }
        xp, *rest_p = _layer(xp, bp, tp, w, None)
        caches = (cache_fox_k[l], cache_fox_v[l], cache_fox_logf[l], cache_sb_k[l], cache_sb_v[l])
        xs, *rest_s = _layer(xs, bs, ts, w, caches)
        outs_p.append(rest_p)
        outs_s.append(rest_s)

    y_prompt = _rmsnorm(xp, final_norm, F32).reshape(bp, tp, d)
    y_sample = _rmsnorm(xs, final_norm, F32).reshape(bs, ts, d)

    def stack(outs, idx, shape):
        return jnp.stack([o[idx].reshape(shape) for o in outs])

    res = [y_prompt, y_sample]
    for outs, b, t in ((outs_p, bp, tp), (outs_s, bs, ts)):
        res += [stack(outs, 0, (b, t, hf, HEAD_DIM)), stack(outs, 1, (b, t, hf, HEAD_DIM)),
                stack(outs, 2, (b, t, hf)),
                stack(outs, 3, (b, t, hs, HEAD_DIM)), stack(outs, 4, (b, t, hs, HEAD_DIM))]
    return tuple(res)
```

```python
import functools
import math

import jax
import jax.numpy as jnp
from jax import lax
from jax.experimental import pallas as pl
from jax.experimental.pallas import tpu as pltpu

EPS = 1e-6
NEG_INF = -1e30
HEAD_DIM = 128
LANES = 128
SUBLANES = 8
LOG2E = math.log2(math.e)
VMEM_LIMIT_BYTES = 56 * 1024 * 1024
SCORE_BUFFERS = 3

F32 = jnp.float32
BF16 = jnp.bfloat16


def _params(*sem):
    return pltpu.CompilerParams(dimension_semantics=sem, vmem_limit_bytes=VMEM_LIMIT_BYTES)


def _pick(n, candidates):
    for c in candidates:
        if n % c == 0:
            return c
    return n


def _softplus(z):
    return jnp.maximum(z, 0.0) + jnp.log1p(jnp.exp(-jnp.abs(z)))


def _softplus2(z):
    return jnp.maximum(z, 0.0) + jnp.log2(1.0 + jnp.exp2(-jnp.abs(z)))


def _dot(a, b):
    return jnp.dot(a, b, preferred_element_type=F32)


def _dot_nt(a, b):
    return lax.dot_general(a, b, (((1,), (1,)), ((), ())), preferred_element_type=F32)


def _split2(x):
    hi = x.astype(BF16)
    lo = (x - hi.astype(F32)).astype(BF16)
    return jnp.concatenate([hi, lo], axis=1)


def _split3(x):
    hi = x.astype(BF16)
    r = x - hi.astype(F32)
    mid = r.astype(BF16)
    lo = (r - mid.astype(F32)).astype(BF16)
    return hi, mid, lo


def _rmsnorm_kernel(x_ref, g_ref, o_ref):
    x = x_ref[...]
    r = lax.rsqrt(jnp.mean(x * x, axis=-1, keepdims=True) + EPS)
    o_ref[...] = (x * r * g_ref[...]).astype(o_ref.dtype)


def _rmsnorm(x, g, out_dtype):
    m, d = x.shape
    tm = _pick(m, (256, 128, 64, 32, 16, 8))
    return pl.pallas_call(
        _rmsnorm_kernel,
        grid=(m // tm,),
        in_specs=[pl.BlockSpec((tm, d), lambda i: (i, 0)),
                  pl.BlockSpec((1, d), lambda i: (0, 0))],
        out_specs=pl.BlockSpec((tm, d), lambda i: (i, 0)),
        out_shape=jax.ShapeDtypeStruct((m, d), out_dtype),
        compiler_params=_params("parallel"),
        name="rmsnorm",
    )(x, g.reshape(1, d).astype(F32))


def _merge_norm_kernel(a_ref, b_ref, ga_ref, gb_ref, o_ref):
    wa = a_ref.shape[1]
    for ref, g_ref, lo in ((a_ref, ga_ref, 0), (b_ref, gb_ref, wa)):
        x = ref[...]
        r = lax.rsqrt(jnp.mean(x * x, axis=-1, keepdims=True) + EPS)
        o_ref[:, lo:lo + x.shape[1]] = (x * r * g_ref[...]).astype(o_ref.dtype)


def _merge_norm(o_fox, o_sb, g_fox, g_sb):
    m, wa = o_fox.shape
    wb = o_sb.shape[1]
    tm = _pick(m, (256, 128, 64, 32, 16, 8))
    return pl.pallas_call(
        _merge_norm_kernel,
        grid=(m // tm,),
        in_specs=[pl.BlockSpec((tm, wa), lambda i: (i, 0)),
                  pl.BlockSpec((tm, wb), lambda i: (i, 0)),
                  pl.BlockSpec((1, wa), lambda i: (0, 0)),
                  pl.BlockSpec((1, wb), lambda i: (0, 0))],
        out_specs=pl.BlockSpec((tm, wa + wb), lambda i: (i, 0)),
        out_shape=jax.ShapeDtypeStruct((m, wa + wb), BF16),
        compiler_params=_params("parallel"),
        name="merge_norm",
    )(o_fox, o_sb, g_fox.reshape(1, wa).astype(F32), g_sb.reshape(1, wb).astype(F32))


def _proj_q_kernel(x_ref, w_ref, *rest, scale, n_side):
    side_in, (o_ref, *side_out) = rest[:n_side], rest[n_side:]
    _side_cast(list(side_in) + side_out)
    o_ref[...] = (_dot_nt(x_ref[...], w_ref[...]) * scale).astype(o_ref.dtype)


def _proj_q(x, w, scale, side, name):
    m, k = x.shape
    n = w.shape[0]
    tm = _pick(m, (1024, 512, 256, 128, 64, 32, 16, 8))
    tn = _pick(n, (1024, 512, 256, 128))
    nj = n // tn
    steps = (m // tm) * nj
    aligned = [row0 % SUBLANES == 0 for _, row0, _ in side]
    plain = [arr[row0:row0 + count].astype(BF16) for (arr, row0, count), ok in zip(side, aligned) if not ok]
    kept = [s for s, ok in zip(side, aligned) if ok]
    side_in, side_out, side_shapes = [], [], []
    for arr, row0, count in kept:
        rows = _side_rows(count, steps)
        last = count // rows - 1
        block = lambda i, j, last=last: jnp.minimum(i * nj + j, last)
        side_in.append(pl.BlockSpec((pl.Element(rows), pl.Element(arr.shape[1])),
                                    lambda i, j, row0=row0, rows=rows, block=block:
                                    (SUBLANES * (row0 // SUBLANES + block(i, j) * (rows // SUBLANES)), 0)))
        side_out.append(pl.BlockSpec((rows, arr.shape[1]), lambda i, j, block=block: (block(i, j), 0)))
        side_shapes.append(jax.ShapeDtypeStruct((count, arr.shape[1]), BF16))
    q, *cast = pl.pallas_call(
        functools.partial(_proj_q_kernel, scale=scale, n_side=len(kept)),
        grid=(m // tm, nj),
        in_specs=[pl.BlockSpec((tm, k), lambda i, j: (i, 0)), pl.BlockSpec((tn, k), lambda i, j: (j, 0))] + side_in,
        out_specs=[pl.BlockSpec((tm, tn), lambda i, j: (i, j))] + side_out,
        out_shape=[jax.ShapeDtypeStruct((m, n), BF16)] + side_shapes,
        compiler_params=_params("arbitrary", "arbitrary"),
        name=name,
    )(x, w, *[arr for arr, _, _ in kept])
    cast, plain = iter(cast), iter(plain)
    return q, [next(cast) if ok else next(plain) for ok in aligned]


def _proj_kv_kernel(x_ref, w_ref, o32_ref, o16_ref, *, nh):
    acc = _dot_nt(x_ref[...], w_ref[...])
    o16_ref[...] = acc.astype(o16_ref.dtype)
    tm = x_ref.shape[0]
    for h in range(nh):
        o32_ref[pl.ds(h, tm, stride=nh), :] = acc[:, h * HEAD_DIM:(h + 1) * HEAD_DIM]


def _proj_kv(x, w, name):
    m, k = x.shape
    n = w.shape[0]
    nh = n // HEAD_DIM
    tm = _pick(m, (512, 256, 128, 64, 32, 16, 8))
    return pl.pallas_call(
        functools.partial(_proj_kv_kernel, nh=nh),
        grid=(m // tm,),
        in_specs=[pl.BlockSpec((tm, k), lambda i: (i, 0)),
                  pl.BlockSpec((n, k), lambda i: (0, 0))],
        out_specs=[pl.BlockSpec((tm * nh, HEAD_DIM), lambda i: (i, 0)),
                   pl.BlockSpec((tm, n), lambda i: (i, 0))],
        out_shape=[jax.ShapeDtypeStruct((m * nh, HEAD_DIM), F32), jax.ShapeDtypeStruct((m, n), BF16)],
        compiler_params=_params("parallel"),
        name=name,
    )(x, w)


def _attn_norm_kernel(x_ref, g_ref, wfl_ref, bfl_ref, *rest, n_side):
    side_in, (h_ref, logf_ref, *side_out) = rest[:n_side], rest[n_side:]
    _side_cast(list(side_in) + side_out)
    x = x_ref[...]
    r = lax.rsqrt(jnp.mean(x * x, axis=-1, keepdims=True) + EPS)
    h = (x * r * g_ref[...]).astype(h_ref.dtype)
    h_ref[...] = h
    logf_ref[...] = -_softplus(-(_dot_nt(h, wfl_ref[...]) + bfl_ref[...]))


def _attn_norm(x, g, w_fl, b_fl, side):
    m, d = x.shape
    tm = _pick(m, (256, 128, 64, 32, 16, 8))
    steps = m // tm
    side_specs, side_shapes = _side_specs(side, steps, lambda i: i)
    row = pl.BlockSpec((tm, d), lambda i: (i, 0))
    h, logf, *cast = pl.pallas_call(
        functools.partial(_attn_norm_kernel, n_side=len(side)),
        grid=(steps,),
        in_specs=[row, pl.BlockSpec((1, d), lambda i: (0, 0)), pl.BlockSpec((LANES, d), lambda i: (0, 0)),
                  pl.BlockSpec((1, LANES), lambda i: (0, 0))] + side_specs,
        out_specs=[row, pl.BlockSpec((tm, LANES), lambda i: (i, 0))] + side_specs,
        out_shape=[jax.ShapeDtypeStruct((m, d), BF16), jax.ShapeDtypeStruct((m, LANES), F32)] + side_shapes,
        compiler_params=_params("arbitrary"),
        name="attn_norm",
    )(x, g.reshape(1, d).astype(F32), w_fl, b_fl, *[s[0] if isinstance(s, tuple) else s for s in side])
    return h, logf, cast


def _proj_resid_kernel(x_ref, w_ref, r_ref, o_ref):
    o_ref[...] = r_ref[...] + _dot(x_ref[...], w_ref[...])


def _proj_swiglu_kernel(x_ref, wg_ref, wu_ref, o_ref):
    x = x_ref[...]
    g = _dot(x, wg_ref[...])
    u = _dot(x, wu_ref[...])
    o_ref[...] = (g * jax.nn.sigmoid(g) * u).astype(o_ref.dtype)


def _matmul_call(kernel, x, weights, extra_row_inputs, extra_tile_inputs, out_dtypes, tn_candidates, name,
                 tm_candidates=(1024, 512, 256, 128, 64, 32, 16, 8)):
    m, k = x.shape
    n = weights[0].shape[1]
    tm = _pick(m, tm_candidates)
    tn = _pick(n, tn_candidates)
    tile = pl.BlockSpec((tm, tn), lambda i, j: (i, j))
    in_specs = [pl.BlockSpec((tm, k), lambda i, j: (i, 0))]
    in_specs += [pl.BlockSpec((k, tn), lambda i, j: (0, j)) for _ in weights]
    in_specs += [pl.BlockSpec((1, tn), lambda i, j: (0, j)) for _ in extra_row_inputs]
    in_specs += [tile for _ in extra_tile_inputs]
    outs = pl.pallas_call(
        kernel,
        grid=(m // tm, n // tn),
        in_specs=in_specs,
        out_specs=[tile for _ in out_dtypes],
        out_shape=[jax.ShapeDtypeStruct((m, n), dt) for dt in out_dtypes],
        compiler_params=_params("parallel", "arbitrary"),
        name=name,
    )(x, *weights, *extra_row_inputs, *extra_tile_inputs)
    return outs


def _cumsum_kernel(x_ref, init_ref, tril_ref, ct_ref, last_ref, carry_sc):
    t = pl.program_id(1)

    @pl.when(t == 0)
    def _():
        carry_sc[...] = init_ref[...]

    parts = jnp.concatenate(_split3(x_ref[...]), axis=1)
    s = _dot(tril_ref[...], parts)
    cum = s[:, :LANES] + s[:, LANES:2 * LANES] + s[:, 2 * LANES:] + carry_sc[...]
    carry_sc[...] = cum[-1:, :]
    ct_ref[...] = cum.T
    last_ref[...] = cum[-1:, :]


def _cumsum_t(x, init):
    b, t, _ = x.shape
    tt = _pick(t, (512, 256, 128))
    idx = jnp.arange(tt)
    tril = (idx[:, None] >= idx[None, :]).astype(BF16)
    return pl.pallas_call(
        _cumsum_kernel,
        grid=(b, t // tt),
        in_specs=[pl.BlockSpec((None, tt, LANES), lambda i, j: (i, j, 0)),
                  pl.BlockSpec((None, 1, LANES), lambda i, j: (i, 0, 0)),
                  pl.BlockSpec((tt, tt), lambda i, j: (0, 0))],
        out_specs=[pl.BlockSpec((None, LANES, tt), lambda i, j: (i, 0, j)),
                   pl.BlockSpec((None, 1, LANES), lambda i, j: (i, 0, 0))],
        out_shape=[jax.ShapeDtypeStruct((b, LANES, t), F32),
                   jax.ShapeDtypeStruct((b, 1, LANES), F32)],
        scratch_shapes=[pltpu.VMEM((1, LANES), F32)],
        compiler_params=_params("parallel", "arbitrary"),
        name="cumsum_t",
    )(x, init, tril)


def _side_rows(k, steps):
    for rows in range(16, k + 1, 16):
        if k % rows == 0 and k // rows <= steps:
            return rows
    return None


def _side_specs(weights, steps, step_index):
    specs, shapes = [], []
    for w in weights:
        w, count = w if isinstance(w, tuple) else (w, w.shape[0])
        rows = _side_rows(count, steps)
        last = count // rows - 1
        specs.append(pl.BlockSpec((rows, w.shape[1]),
                                  lambda *g, last=last: (jnp.minimum(step_index(*g), last), 0)))
        shapes.append(jax.ShapeDtypeStruct((count, w.shape[1]), BF16))
    return specs, shapes


def _side_cast(refs):
    n = len(refs) // 2
    for w_ref, o_ref in zip(refs[:n], refs[n:]):
        o_ref[...] = w_ref[...].astype(o_ref.dtype)


def _fox_prompt_kernel(q_ref, k_ref, v_ref, ck_ref, *rest, tq, tk, n_side):
    side_in, (o_ref, *side_out) = rest[:n_side], rest[n_side:2 * n_side + 1]
    kaug_sc, vaug_sc, m_sc, acc_sc, s_sc = rest[2 * n_side + 1:]
    _side_cast(list(side_in) + side_out)
    t = k_ref.shape[0]
    nq = t // tq
    nbuf = s_sc.shape[0]
    hi, mid, lo = _split3(ck_ref[...] * LOG2E)
    rows = lax.broadcasted_iota(jnp.int32, (HEAD_DIM, t), 0)
    parts = jnp.where(rows == 0, hi.astype(F32),
                      jnp.where(rows == 1, mid.astype(F32), jnp.where(rows == 2, lo.astype(F32), 0.0)))
    kaug_sc[:, :HEAD_DIM] = k_ref[...]
    kaug_sc[:, HEAD_DIM:] = parts.T.astype(BF16)
    lane = lax.broadcasted_iota(jnp.int32, (t, HEAD_DIM), 1)
    vaug_sc[:, :HEAD_DIM] = v_ref[...]
    vaug_sc[:, HEAD_DIM:] = jnp.where(lane == 0, 1.0, 0.0).astype(BF16)

    lane_q = lax.broadcasted_iota(jnp.int32, (tq, HEAD_DIM), 1)
    q_extra = jnp.where(lane_q < 3, -1.0, 0.0).astype(BF16)
    nc = tk // LANES
    blocks = [(qi, kb) for qi in range(nq) for kb in range((qi * tq) // tk + 1)]

    def scores(n):
        qi, kb = blocks[n]
        qa = jnp.concatenate([q_ref[qi * tq:(qi + 1) * tq, :], q_extra], axis=1)
        s_sc[n % nbuf] = _dot_nt(qa, kaug_sc[kb * tk:(kb + 1) * tk, :])

    def softmax_pv(n):
        qi, kb = blocks[n]
        first = kb == 0
        last = kb == (qi * tq) // tk
        s = s_sc[n % nbuf]
        if last:
            row = qi * tq + lax.broadcasted_iota(jnp.int32, s.shape, 0)
            col = kb * tk + lax.broadcasted_iota(jnp.int32, s.shape, 1)
            s = jnp.where(col <= row, s, NEG_INF)
        chunks = [s[:, c * LANES:(c + 1) * LANES] for c in range(nc)]
        bm = jnp.max(functools.reduce(jnp.maximum, chunks), axis=-1, keepdims=True)
        if first:
            m_new = jnp.broadcast_to(bm, (tq, LANES))
        else:
            m_prev = m_sc[qi]
            m_new = jnp.maximum(m_prev, bm)
            alpha = jnp.exp2(m_prev - m_new)
        p = jnp.concatenate([jnp.exp2(c - m_new).astype(BF16) for c in chunks], axis=1)
        pv = _dot(p, vaug_sc[kb * tk:(kb + 1) * tk, :])
        acc = pv if first else jnp.concatenate([alpha, alpha], axis=1) * acc_sc[qi] + pv
        if last:
            o_ref[qi * tq:(qi + 1) * tq, :] = acc[:, :HEAD_DIM] / acc[:, HEAD_DIM:HEAD_DIM + 1]
        else:
            acc_sc[qi] = acc
            m_sc[qi] = m_new

    scores(0)
    for n in range(len(blocks)):
        if n + 1 < len(blocks):
            scores(n + 1)
        softmax_pv(n)


def _fox_prompt(q, k, v, ck, side, b, t, h):
    tq = _pick(t, (512, 256, 128))
    nq = t // tq
    blk = pl.BlockSpec((t, HEAD_DIM), lambda i, j: (i, j))
    side_specs, side_shapes = _side_specs(side, b * h, lambda i, j: i * h + j)
    o, *cast = pl.pallas_call(
        functools.partial(_fox_prompt_kernel, tq=tq, tk=tq, n_side=len(side)),
        grid=(b, h),
        in_specs=[blk, blk, blk, pl.BlockSpec((None, None, 1, t), lambda i, j: (i, j, 0, 0))] + side_specs,
        out_specs=[blk] + side_specs,
        out_shape=[jax.ShapeDtypeStruct((b * t, h * HEAD_DIM), F32)] + side_shapes,
        scratch_shapes=[pltpu.VMEM((t, 2 * HEAD_DIM), BF16), pltpu.VMEM((t, 2 * HEAD_DIM), BF16),
                        pltpu.VMEM((nq, tq, LANES), F32), pltpu.VMEM((nq, tq, 2 * HEAD_DIM), F32),
                        pltpu.VMEM((SCORE_BUFFERS, tq, tq), F32)],
        compiler_params=_params("arbitrary", "arbitrary"),
        name="fox_prompt",
    )(q, k, v, ck, *side)
    return o, cast


def _tri2(tk):
    idx = jnp.arange(tk)
    tri = (idx[:, None] >= idx[None, :]).astype(BF16)
    return jnp.concatenate([tri, tri], axis=0)


SKIP_LOG2 = 152.0


def _sb_prompt_kernel(q_ref, k_ref, v_ref, tri2_ref, *rest, tb, n_side):
    side_in, (o_ref, *side_out) = rest[:n_side], rest[n_side:2 * n_side + 1]
    carry_sc, acc_sc, z_sc = rest[2 * n_side + 1:]
    _side_cast(list(side_in) + side_out)
    t = k_ref.shape[0]
    nq = t // tb
    nbuf = z_sc.shape[0]
    tri2 = tri2_ref[...]
    nc = tb // LANES

    def q_tile(qi):
        return q_ref[qi * tb:(qi + 1) * tb, :]

    def weights(z, carry, diagonal):
        sp = _softplus2(z)
        if diagonal:
            row = lax.broadcasted_iota(jnp.int32, z.shape, 0)
            col = lax.broadcasted_iota(jnp.int32, z.shape, 1)
            valid = col < row
            sp = jnp.where(valid, sp, 0.0)
        incl = _dot(_split2(sp), tri2)
        arg = z - incl
        if carry is not None:
            arg = arg - jnp.concatenate([carry] * nc, axis=1)
        a = jnp.exp2(arg)
        if diagonal:
            a = jnp.where(valid, a, 0.0)
        tot = incl[:, :1]
        new_carry = jnp.broadcast_to(tot, (tb, LANES)) if carry is None else carry + tot
        return a.astype(BF16), new_carry

    blocks = [(qi, kb) for qi in range(nq) for kb in (qi, qi - 1) if kb >= 0]

    def scores(n):
        qi, kb = blocks[n]
        z_sc[n % nbuf] = _dot_nt(q_tile(qi), k_ref[kb * tb:(kb + 1) * tb, :])

    def near(n):
        qi, kb = blocks[n]
        first = kb == qi
        a, carry = weights(z_sc[n % nbuf], None if first else carry_sc[qi], first)
        pv = _dot(a, v_ref[kb * tb:(kb + 1) * tb, :])
        acc = pv if first else acc_sc[qi] + pv
        if kb == qi - 1 or kb == 0:
            o_ref[qi * tb:(qi + 1) * tb, :] = acc
        if kb > 0:
            acc_sc[qi] = acc
            carry_sc[qi] = carry

    scores(0)
    for n in range(len(blocks)):
        if n + 1 < len(blocks):
            scores(n + 1)
        near(n)

    if nq <= 2:
        return
    lightest = functools.reduce(jnp.minimum, [carry_sc[qi] for qi in range(2, nq)])

    @pl.when(jnp.min(lightest) <= SKIP_LOG2)
    def _():
        for qi in range(2, nq):
            def more(kb):
                return jnp.logical_and(kb >= 0, jnp.min(carry_sc[qi]) <= SKIP_LOG2)

            def far(kb):
                start = pl.multiple_of(kb * tb, tb)
                z = _dot_nt(q_tile(qi), k_ref[pl.ds(start, tb), :])
                a, carry = weights(z, carry_sc[qi], False)
                acc_sc[qi] = acc_sc[qi] + _dot(a, v_ref[pl.ds(start, tb), :])
                carry_sc[qi] = carry
                return kb - 1

            lax.while_loop(more, far, qi - 2)
            o_ref[qi * tb:(qi + 1) * tb, :] = acc_sc[qi]


def _sb_prompt(q, k, v, side, b, t, h):
    tb = _pick(t, (256, 128))
    nq = t // tb
    blk = pl.BlockSpec((t, HEAD_DIM), lambda i, j: (i, j))
    side_specs, side_shapes = _side_specs(side, b * h, lambda i, j: i * h + j)
    o, *cast = pl.pallas_call(
        functools.partial(_sb_prompt_kernel, tb=tb, n_side=len(side)),
        grid=(b, h),
        in_specs=[blk, blk, blk, pl.BlockSpec((2 * tb, tb), lambda i, j: (0, 0))] + side_specs,
        out_specs=[blk] + side_specs,
        out_shape=[jax.ShapeDtypeStruct((b * t, h * HEAD_DIM), F32)] + side_shapes,
        scratch_shapes=[pltpu.VMEM((nq, tb, LANES), F32), pltpu.VMEM((nq, tb, HEAD_DIM), F32),
                        pltpu.VMEM((SCORE_BUFFERS, tb, tb), F32)],
        compiler_params=_params("arbitrary", "arbitrary"),
        name="sb_prompt",
    )(q, k, v, _tri2(tb), *side)
    return o, cast


def _head_rows(ref, h, nh, past):
    return ref[pl.ds(h, past, stride=nh), :].astype(BF16)


def _fox_sample_kernel(q_ref, kc_ref, vc_ref, kn_ref, vn_ref, ckc_ref, ckn_ref, o_ref, *, nh, past):
    ts = q_ref.shape[0]
    row = lax.broadcasted_iota(jnp.int32, (ts, ts), 0)
    col = lax.broadcasted_iota(jnp.int32, (ts, ts), 1)
    for h in range(nh):
        sl = slice(h * HEAD_DIM, (h + 1) * HEAD_DIM)
        q = q_ref[:, sl]
        s1 = _dot_nt(q, _head_rows(kc_ref, h, nh, past)) - ckc_ref[h:h + 1, :] * LOG2E
        s2 = _dot_nt(q, kn_ref[:, sl]) - ckn_ref[h:h + 1, :] * LOG2E
        s2 = jnp.where(col <= row, s2, NEG_INF)
        m = jnp.maximum(jnp.max(s1, axis=-1, keepdims=True), jnp.max(s2, axis=-1, keepdims=True))
        p1 = jnp.exp2(s1 - m)
        p2 = jnp.exp2(s2 - m)
        l = jnp.sum(p1, axis=-1, keepdims=True) + jnp.sum(p2, axis=-1, keepdims=True)
        acc = _dot(p1.astype(BF16), _head_rows(vc_ref, h, nh, past)) + _dot(p2.astype(BF16), vn_ref[:, sl])
        o_ref[:, sl] = acc / l


def _sb_sample_kernel(q_ref, kc_ref, vc_ref, kn_ref, vn_ref, tric_ref, trin_ref, o_ref, *, nh, past, tc):
    ts = q_ref.shape[0]
    row = lax.broadcasted_iota(jnp.int32, (ts, ts), 0)
    col = lax.broadcasted_iota(jnp.int32, (ts, ts), 1)
    valid = col < row
    tric = tric_ref[...]
    trin = trin_ref[...]
    nchunk = past // tc
    for h in range(nh):
        sl = slice(h * HEAD_DIM, (h + 1) * HEAD_DIM)
        q = q_ref[:, sl]
        vc = _head_rows(vc_ref, h, nh, past)
        z2 = _dot_nt(q, kn_ref[:, sl])
        incl2 = _dot(_split2(jnp.where(valid, _softplus2(z2), 0.0)), trin)
        a2 = jnp.where(valid, jnp.exp2(z2 - incl2), 0.0)
        acc = _dot(a2.astype(BF16), vn_ref[:, sl])
        carry = incl2[:, :1]
        z1 = _dot_nt(q, _head_rows(kc_ref, h, nh, past))
        sp1 = _softplus2(z1)
        stacked = jnp.concatenate([sp1[:, c * tc:(c + 1) * tc] for c in range(nchunk)], axis=0)
        incl = _dot(_split2(stacked), tric)
        for c in reversed(range(nchunk)):
            inc_c = incl[c * ts:(c + 1) * ts, :]
            a = jnp.exp2(z1[:, c * tc:(c + 1) * tc] - inc_c - carry)
            acc = acc + _dot(a.astype(BF16), vc[c * tc:(c + 1) * tc, :])
            carry = carry + inc_c[:, :1]
        o_ref[:, sl] = acc


def _sample_specs(ts, past, nh):
    new = pl.BlockSpec((ts, nh * HEAD_DIM), lambda i: (i, 0))
    cache = pl.BlockSpec((None, past * nh, HEAD_DIM), lambda i: (i, 0, 0))
    return new, cache


def _fox_sample(q, kc, vc, kn, vn, ckc, ckn, b, ts, past, h):
    new, cache = _sample_specs(ts, past, h)
    return pl.pallas_call(
        functools.partial(_fox_sample_kernel, nh=h, past=past),
        grid=(b,),
        in_specs=[new, cache, cache, new, new,
                  pl.BlockSpec((None, h, past), lambda i: (i, 0, 0)),
                  pl.BlockSpec((None, h, ts), lambda i: (i, 0, 0))],
        out_specs=new,
        out_shape=jax.ShapeDtypeStruct((b * ts, h * HEAD_DIM), F32),
        compiler_params=_params("parallel"),
        name="fox_sample",
    )(q, kc, vc, kn, vn, ckc, ckn)


def _sb_sample(q, kc, vc, kn, vn, b, ts, past, h):
    new, cache = _sample_specs(ts, past, h)
    tc = _pick(past, (256, 128))
    return pl.pallas_call(
        functools.partial(_sb_sample_kernel, nh=h, past=past, tc=tc),
        grid=(b,),
        in_specs=[new, cache, cache, new, new,
                  pl.BlockSpec((2 * tc, tc), lambda i: (0, 0)),
                  pl.BlockSpec((2 * ts, ts), lambda i: (0, 0))],
        out_specs=new,
        out_shape=jax.ShapeDtypeStruct((b * ts, h * HEAD_DIM), F32),
        compiler_params=_params("parallel"),
        name="sb_sample",
    )(q, kc, vc, kn, vn, _tri2(tc), _tri2(ts))


def _layer(x, b, t, w, caches):
    hf = w["b_forget"].shape[0]
    hs = w["h_sb"]
    q_scale = HEAD_DIM ** -0.5 * LOG2E
    big = (1024, 512, 256, 128)

    first = [(w["w_in_t"], hf * HEAD_DIM)] if caches is None else []
    hn, logf_pad, cast = _attn_norm(x, w["attn_norm"], w["w_fl"], w["b_fl"], first)
    if first:
        w["w_q_fox"] = cast[0]
    qkv = {}
    for grp in ("fox", "sb"):
        pending = w["to_cast_" + grp] if caches is None else []
        q, cast = _proj_q(hn, w["w_q_" + grp], q_scale,
                          [(w["w_in_t"], row0, count) for _, row0, count in pending], "proj_q_" + grp)
        for (key, _, _), arr in zip(pending, cast):
            w[key] = arr
        wk, wv = w["w_k_" + grp], w["w_v_" + grp]
        k32, k16 = _proj_kv(hn, wk, "proj_k_" + grp)
        v32, v16 = _proj_kv(hn, wv, "proj_v_" + grp)
        qkv[grp] = (q, k32, k16, v32, v16)

    qf, kf32, kf, vf32, vf = qkv["fox"]
    qs, ks32, ks, vs32, vs = qkv["sb"]
    zeros_init = jnp.zeros((b, 1, LANES), F32)
    if caches is None:
        ct, _ = _cumsum_t(logf_pad.reshape(b, t, LANES), zeros_init)
        o_fox, (w["w_gate"], w["w_up"]) = _fox_prompt(qf, kf, vf, ct[:, :hf, :].reshape(b, hf, 1, t),
                                                       [w["w_gate_f32"], w["w_up_f32"]], b, t, hf)
        o_sb, (w["w_down"], w["w_out"]) = _sb_prompt(qs, ks, vs, [w["w_down_f32"], w["w_out_f32"]], b, t, hs)
    else:
        c_fk, c_fv, c_fl, c_sk, c_sv = caches
        past = c_fk.shape[1]
        c_fl_pad = jnp.pad(c_fl.astype(F32), ((0, 0), (0, 0), (0, LANES - hf)))
        ct_c, last = _cumsum_t(c_fl_pad, zeros_init)
        ct_n, _ = _cumsum_t(logf_pad.reshape(b, t, LANES), last)
        o_fox = _fox_sample(qf, c_fk.reshape(b, past * hf, HEAD_DIM), c_fv.reshape(b, past * hf, HEAD_DIM), kf, vf,
                            ct_c[:, :hf, :], ct_n[:, :hf, :], b, t, past, hf)
        o_sb = _sb_sample(qs, c_sk.reshape(b, past * hs, HEAD_DIM), c_sv.reshape(b, past * hs, HEAD_DIM), ks, vs,
                          b, t, past, hs)

    a = _merge_norm(o_fox, o_sb, w["out_norm_fox"], w["out_norm_sb"])
    (x1,) = _matmul_call(_proj_resid_kernel, a, [w["w_out"]], [], [x], [F32], big, "proj_out")
    h2 = _rmsnorm(x1, w["ffn_norm"], BF16)
    (act,) = _matmul_call(_proj_swiglu_kernel, h2, [w["w_gate"], w["w_up"]], [], [], [BF16], (256, 128), "ffn_up",
                          tm_candidates=(2048, 1024, 512, 256, 128, 64, 32, 16, 8))
    (x2,) = _matmul_call(_proj_resid_kernel, act, [w["w_down"]], [], [x1], [F32], (512, 256, 128), "ffn_down",
                         tm_candidates=(512, 256, 128, 64, 32, 16, 8))
    return x2, kf32, vf32, logf_pad[:, :hf], ks32, vs32


def kernel(x_prompt, x_sample, cache_fox_k, cache_fox_v, cache_fox_logf, cache_sb_k, cache_sb_v, attn_norm, w_in,
           b_forget, out_norm_fox, out_norm_sb, w_out, ffn_norm, w_gate, w_up, w_down, final_norm):
    bp, tp, d = x_prompt.shape
    bs, ts, _ = x_sample.shape
    depth = w_in.shape[0]
    hf = cache_fox_k.shape[3]
    hs = cache_sb_k.shape[3]
    w_fox = hf * HEAD_DIM
    w_sb = hs * HEAD_DIM

    xp = x_prompt.reshape(bp * tp, d)
    xs = x_sample.reshape(bs * ts, d)
    outs_p, outs_s = [], []
    for l in range(depth):
        cuts = [0, w_fox, 2 * w_fox, 3 * w_fox, 3 * w_fox + hf, 3 * w_fox + hf + w_sb, 3 * w_fox + hf + 2 * w_sb,
                3 * w_fox + hf + 3 * w_sb]
        wl = jnp.swapaxes(w_in[l], 0, 1)
        w = {
            "attn_norm": attn_norm[l], "ffn_norm": ffn_norm[l],
            "out_norm_fox": out_norm_fox[l], "out_norm_sb": out_norm_sb[l],
            "h_sb": hs, "w_in_t": wl,
            "to_cast_fox": [("w_k_fox", cuts[1], w_fox), ("w_v_fox", cuts[2], w_fox), ("w_q_sb", cuts[4], w_sb)],
            "to_cast_sb": [("w_k_sb", cuts[5], w_sb), ("w_v_sb", cuts[6], w_sb)],
            "w_fl": jnp.pad(wl[cuts[3]:cuts[4], :], ((0, LANES - hf), (0, 0))).astype(BF16),
            "b_fl": jnp.pad(b_forget[l].astype(F32), (0, LANES - hf)).reshape(1, LANES),
            "b_forget": b_forget[l],
            "w_out_f32": w_out[l], "w_gate_f32": w_gate[l], "w_up_f32": w_up[l], "w_down_f32": w_down[l],
        }
        xp, *rest_p = _layer(xp, bp, tp, w, None)
        caches = (cache_fox_k[l], cache_fox_v[l], cache_fox_logf[l], cache_sb_k[l], cache_sb_v[l])
        xs, *rest_s = _layer(xs, bs, ts, w, caches)
        outs_p.append(rest_p)
        outs_s.append(rest_s)

    y_prompt = _rmsnorm(xp, final_norm, F32).reshape(bp, tp, d)
    y_sample = _rmsnorm(xs, final_norm, F32).reshape(bs, ts, d)

    def stack(outs, idx, shape):
        return jnp.stack([o[idx].reshape(shape) for o in outs])

    res = [y_prompt, y_sample]
    for outs, b, t in ((outs_p, bp, tp), (outs_s, bs, ts)):
        res += [stack(outs, 0, (b, t, hf, HEAD_DIM)), stack(outs, 1, (b, t, hf, HEAD_DIM)),
                stack(outs, 2, (b, t, hf)),
                stack(outs, 3, (b, t, hs, HEAD_DIM)), stack(outs, 4, (b, t, hs, HEAD_DIM))]
    return tuple(res)
```

```python
import functools
import math

import jax
import jax.numpy as jnp
from jax import lax
from jax.experimental import pallas as pl
from jax.experimental.pallas import tpu as pltpu

EPS = 1e-6
NEG_INF = -1e30
HEAD_DIM = 128
LANES = 128
SUBLANES = 8
LOG2E = math.log2(math.e)
VMEM_LIMIT_BYTES = 56 * 1024 * 1024
SCORE_BUFFERS = 3
HEADS_PER_STEP = 2

F32 = jnp.float32
BF16 = jnp.bfloat16


def _params(*sem):
    return pltpu.CompilerParams(dimension_semantics=sem, vmem_limit_bytes=VMEM_LIMIT_BYTES)


def _pick(n, candidates):
    for c in candidates:
        if n % c == 0:
            return c
    return n


def _softplus(z):
    return jnp.maximum(z, 0.0) + jnp.log1p(jnp.exp(-jnp.abs(z)))


def _softplus2(z):
    return jnp.maximum(z, 0.0) + jnp.log2(1.0 + jnp.exp2(-jnp.abs(z)))


def _dot(a, b):
    return jnp.dot(a, b, preferred_element_type=F32)


def _dot_nt(a, b):
    return lax.dot_general(a, b, (((1,), (1,)), ((), ())), preferred_element_type=F32)


def _split2(x):
    hi = x.astype(BF16)
    lo = (x - hi.astype(F32)).astype(BF16)
    return jnp.concatenate([hi, lo], axis=1)


def _split3(x):
    hi = x.astype(BF16)
    r = x - hi.astype(F32)
    mid = r.astype(BF16)
    lo = (r - mid.astype(F32)).astype(BF16)
    return hi, mid, lo


def _rmsnorm_kernel(x_ref, g_ref, o_ref):
    x = x_ref[...]
    r = lax.rsqrt(jnp.mean(x * x, axis=-1, keepdims=True) + EPS)
    o_ref[...] = (x * r * g_ref[...]).astype(o_ref.dtype)


def _rmsnorm(x, g, out_dtype):
    m, d = x.shape
    tm = _pick(m, (256, 128, 64, 32, 16, 8))
    return pl.pallas_call(
        _rmsnorm_kernel,
        grid=(m // tm,),
        in_specs=[pl.BlockSpec((tm, d), lambda i: (i, 0)),
                  pl.BlockSpec((1, d), lambda i: (0, 0))],
        out_specs=pl.BlockSpec((tm, d), lambda i: (i, 0)),
        out_shape=jax.ShapeDtypeStruct((m, d), out_dtype),
        compiler_params=_params("parallel"),
        name="rmsnorm",
    )(x, g.reshape(1, d).astype(F32))


def _merge_norm_kernel(a_ref, b_ref, ga_ref, gb_ref, o_ref):
    wa = a_ref.shape[1]
    for ref, g_ref, lo in ((a_ref, ga_ref, 0), (b_ref, gb_ref, wa)):
        x = ref[...]
        r = lax.rsqrt(jnp.mean(x * x, axis=-1, keepdims=True) + EPS)
        o_ref[:, lo:lo + x.shape[1]] = (x * r * g_ref[...]).astype(o_ref.dtype)


def _merge_norm(o_fox, o_sb, g_fox, g_sb):
    m, wa = o_fox.shape
    wb = o_sb.shape[1]
    tm = _pick(m, (256, 128, 64, 32, 16, 8))
    return pl.pallas_call(
        _merge_norm_kernel,
        grid=(m // tm,),
        in_specs=[pl.BlockSpec((tm, wa), lambda i: (i, 0)),
                  pl.BlockSpec((tm, wb), lambda i: (i, 0)),
                  pl.BlockSpec((1, wa), lambda i: (0, 0)),
                  pl.BlockSpec((1, wb), lambda i: (0, 0))],
        out_specs=pl.BlockSpec((tm, wa + wb), lambda i: (i, 0)),
        out_shape=jax.ShapeDtypeStruct((m, wa + wb), BF16),
        compiler_params=_params("parallel"),
        name="merge_norm",
    )(o_fox, o_sb, g_fox.reshape(1, wa).astype(F32), g_sb.reshape(1, wb).astype(F32))


def _proj_q_kernel(x_ref, w_ref, *rest, scale, n_side):
    side_in, (o_ref, *side_out) = rest[:n_side], rest[n_side:]
    _side_cast(list(side_in) + side_out)
    o_ref[...] = (_dot_nt(x_ref[...], w_ref[...]) * scale).astype(o_ref.dtype)


def _proj_q(x, w, scale, side, name):
    m, k = x.shape
    n = w.shape[0]
    tm = _pick(m, (1024, 512, 256, 128, 64, 32, 16, 8))
    tn = _pick(n, (1024, 512, 256, 128))
    nj = n // tn
    steps = (m // tm) * nj
    aligned = [row0 % SUBLANES == 0 for _, row0, _ in side]
    plain = [arr[row0:row0 + count].astype(BF16) for (arr, row0, count), ok in zip(side, aligned) if not ok]
    kept = [s for s, ok in zip(side, aligned) if ok]
    side_in, side_out, side_shapes = [], [], []
    for arr, row0, count in kept:
        rows = _side_rows(count, steps)
        last = count // rows - 1
        block = lambda i, j, last=last: jnp.minimum(i * nj + j, last)
        side_in.append(pl.BlockSpec((pl.Element(rows), pl.Element(arr.shape[1])),
                                    lambda i, j, row0=row0, rows=rows, block=block:
                                    (SUBLANES * (row0 // SUBLANES + block(i, j) * (rows // SUBLANES)), 0)))
        side_out.append(pl.BlockSpec((rows, arr.shape[1]), lambda i, j, block=block: (block(i, j), 0)))
        side_shapes.append(jax.ShapeDtypeStruct((count, arr.shape[1]), BF16))
    q, *cast = pl.pallas_call(
        functools.partial(_proj_q_kernel, scale=scale, n_side=len(kept)),
        grid=(m // tm, nj),
        in_specs=[pl.BlockSpec((tm, k), lambda i, j: (i, 0)), pl.BlockSpec((tn, k), lambda i, j: (j, 0))] + side_in,
        out_specs=[pl.BlockSpec((tm, tn), lambda i, j: (i, j))] + side_out,
        out_shape=[jax.ShapeDtypeStruct((m, n), BF16)] + side_shapes,
        compiler_params=_params("arbitrary", "arbitrary"),
        name=name,
    )(x, w, *[arr for arr, _, _ in kept])
    cast, plain = iter(cast), iter(plain)
    return q, [next(cast) if ok else next(plain) for ok in aligned]


def _proj_kv_kernel(x_ref, w_ref, o32_ref, o16_ref, *, nh):
    acc = _dot_nt(x_ref[...], w_ref[...])
    o16_ref[...] = acc.astype(o16_ref.dtype)
    tm = x_ref.shape[0]
    for h in range(nh):
        o32_ref[pl.ds(h, tm, stride=nh), :] = acc[:, h * HEAD_DIM:(h + 1) * HEAD_DIM]


def _proj_kv(x, w, name):
    m, k = x.shape
    n = w.shape[0]
    nh = n // HEAD_DIM
    tm = _pick(m, (512, 256, 128, 64, 32, 16, 8))
    return pl.pallas_call(
        functools.partial(_proj_kv_kernel, nh=nh),
        grid=(m // tm,),
        in_specs=[pl.BlockSpec((tm, k), lambda i: (i, 0)),
                  pl.BlockSpec((n, k), lambda i: (0, 0))],
        out_specs=[pl.BlockSpec((tm * nh, HEAD_DIM), lambda i: (i, 0)),
                   pl.BlockSpec((tm, n), lambda i: (i, 0))],
        out_shape=[jax.ShapeDtypeStruct((m * nh, HEAD_DIM), F32), jax.ShapeDtypeStruct((m, n), BF16)],
        compiler_params=_params("parallel"),
        name=name,
    )(x, w)


def _attn_norm_kernel(x_ref, g_ref, wfl_ref, bfl_ref, *rest, n_side):
    side_in, (h_ref, logf_ref, *side_out) = rest[:n_side], rest[n_side:]
    _side_cast(list(side_in) + side_out)
    x = x_ref[...]
    r = lax.rsqrt(jnp.mean(x * x, axis=-1, keepdims=True) + EPS)
    h = (x * r * g_ref[...]).astype(h_ref.dtype)
    h_ref[...] = h
    logf_ref[...] = -_softplus(-(_dot_nt(h, wfl_ref[...]) + bfl_ref[...]))


def _attn_norm(x, g, w_fl, b_fl, side):
    m, d = x.shape
    tm = _pick(m, (256, 128, 64, 32, 16, 8))
    steps = m // tm
    side_specs, side_shapes = _side_specs(side, steps, lambda i: i)
    row = pl.BlockSpec((tm, d), lambda i: (i, 0))
    h, logf, *cast = pl.pallas_call(
        functools.partial(_attn_norm_kernel, n_side=len(side)),
        grid=(steps,),
        in_specs=[row, pl.BlockSpec((1, d), lambda i: (0, 0)), pl.BlockSpec((LANES, d), lambda i: (0, 0)),
                  pl.BlockSpec((1, LANES), lambda i: (0, 0))] + side_specs,
        out_specs=[row, pl.BlockSpec((tm, LANES), lambda i: (i, 0))] + side_specs,
        out_shape=[jax.ShapeDtypeStruct((m, d), BF16), jax.ShapeDtypeStruct((m, LANES), F32)] + side_shapes,
        compiler_params=_params("arbitrary"),
        name="attn_norm",
    )(x, g.reshape(1, d).astype(F32), w_fl, b_fl, *[s[0] if isinstance(s, tuple) else s for s in side])
    return h, logf, cast


def _proj_resid_kernel(x_ref, w_ref, r_ref, o_ref):
    o_ref[...] = r_ref[...] + _dot(x_ref[...], w_ref[...])


def _proj_swiglu_kernel(x_ref, wg_ref, wu_ref, o_ref):
    x = x_ref[...]
    g = _dot(x, wg_ref[...])
    u = _dot(x, wu_ref[...])
    o_ref[...] = (g * jax.nn.sigmoid(g) * u).astype(o_ref.dtype)


def _matmul_call(kernel, x, weights, extra_row_inputs, extra_tile_inputs, out_dtypes, tn_candidates, name,
                 tm_candidates=(1024, 512, 256, 128, 64, 32, 16, 8)):
    m, k = x.shape
    n = weights[0].shape[1]
    tm = _pick(m, tm_candidates)
    tn = _pick(n, tn_candidates)
    tile = pl.BlockSpec((tm, tn), lambda i, j: (i, j))
    in_specs = [pl.BlockSpec((tm, k), lambda i, j: (i, 0))]
    in_specs += [pl.BlockSpec((k, tn), lambda i, j: (0, j)) for _ in weights]
    in_specs += [pl.BlockSpec((1, tn), lambda i, j: (0, j)) for _ in extra_row_inputs]
    in_specs += [tile for _ in extra_tile_inputs]
    outs = pl.pallas_call(
        kernel,
        grid=(m // tm, n // tn),
        in_specs=in_specs,
        out_specs=[tile for _ in out_dtypes],
        out_shape=[jax.ShapeDtypeStruct((m, n), dt) for dt in out_dtypes],
        compiler_params=_params("parallel", "arbitrary"),
        name=name,
    )(x, *weights, *extra_row_inputs, *extra_tile_inputs)
    return outs


def _cumsum_kernel(x_ref, init_ref, tril_ref, ct_ref, last_ref, carry_sc):
    t = pl.program_id(1)

    @pl.when(t == 0)
    def _():
        carry_sc[...] = init_ref[...]

    parts = jnp.concatenate(_split3(x_ref[...]), axis=1)
    s = _dot(tril_ref[...], parts)
    cum = s[:, :LANES] + s[:, LANES:2 * LANES] + s[:, 2 * LANES:] + carry_sc[...]
    carry_sc[...] = cum[-1:, :]
    ct_ref[...] = cum.T
    last_ref[...] = cum[-1:, :]


def _cumsum_t(x, init):
    b, t, _ = x.shape
    tt = _pick(t, (512, 256, 128))
    idx = jnp.arange(tt)
    tril = (idx[:, None] >= idx[None, :]).astype(BF16)
    return pl.pallas_call(
        _cumsum_kernel,
        grid=(b, t // tt),
        in_specs=[pl.BlockSpec((None, tt, LANES), lambda i, j: (i, j, 0)),
                  pl.BlockSpec((None, 1, LANES), lambda i, j: (i, 0, 0)),
                  pl.BlockSpec((tt, tt), lambda i, j: (0, 0))],
        out_specs=[pl.BlockSpec((None, LANES, tt), lambda i, j: (i, 0, j)),
                   pl.BlockSpec((None, 1, LANES), lambda i, j: (i, 0, 0))],
        out_shape=[jax.ShapeDtypeStruct((b, LANES, t), F32),
                   jax.ShapeDtypeStruct((b, 1, LANES), F32)],
        scratch_shapes=[pltpu.VMEM((1, LANES), F32)],
        compiler_params=_params("parallel", "arbitrary"),
        name="cumsum_t",
    )(x, init, tril)


def _side_rows(k, steps):
    for rows in range(16, k + 1, 16):
        if k % rows == 0 and k // rows <= steps:
            return rows
    return None


def _side_specs(weights, steps, step_index):
    specs, shapes = [], []
    for w in weights:
        w, count = w if isinstance(w, tuple) else (w, w.shape[0])
        rows = _side_rows(count, steps)
        last = count // rows - 1
        specs.append(pl.BlockSpec((rows, w.shape[1]),
                                  lambda *g, last=last: (jnp.minimum(step_index(*g), last), 0)))
        shapes.append(jax.ShapeDtypeStruct((count, w.shape[1]), BF16))
    return specs, shapes


def _side_cast(refs):
    n = len(refs) // 2
    for w_ref, o_ref in zip(refs[:n], refs[n:]):
        o_ref[...] = w_ref[...].astype(o_ref.dtype)


def _fox_prompt_kernel(q_ref, k_ref, v_ref, ck_ref, *rest, tq, tk, n_side):
    side_in, (o_ref, *side_out) = rest[:n_side], rest[n_side:2 * n_side + 1]
    kaug_sc, vaug_sc, m_sc, acc_sc, s_sc = rest[2 * n_side + 1:]
    _side_cast(list(side_in) + side_out)
    t = k_ref.shape[0]
    nq = t // tq
    nbuf = s_sc.shape[0]
    nhead = kaug_sc.shape[0]
    rows = lax.broadcasted_iota(jnp.int32, (HEAD_DIM, t), 0)
    lane = lax.broadcasted_iota(jnp.int32, (t, HEAD_DIM), 1)
    for hh in range(nhead):
        hi, mid, lo = _split3(ck_ref[hh] * LOG2E)
        parts = jnp.where(rows == 0, hi.astype(F32),
                          jnp.where(rows == 1, mid.astype(F32), jnp.where(rows == 2, lo.astype(F32), 0.0)))
        kaug_sc[hh, :, :HEAD_DIM] = k_ref[:, hh * HEAD_DIM:(hh + 1) * HEAD_DIM]
        kaug_sc[hh, :, HEAD_DIM:] = parts.T.astype(BF16)
        vaug_sc[hh, :, :HEAD_DIM] = v_ref[:, hh * HEAD_DIM:(hh + 1) * HEAD_DIM]
        vaug_sc[hh, :, HEAD_DIM:] = jnp.where(lane == 0, 1.0, 0.0).astype(BF16)

    lane_q = lax.broadcasted_iota(jnp.int32, (tq, HEAD_DIM), 1)
    q_extra = jnp.where(lane_q < 3, -1.0, 0.0).astype(BF16)
    nc = tk // LANES
    blocks = [(hh, qi, kb) for qi in range(nq) for kb in range((qi * tq) // tk + 1) for hh in range(nhead)]

    def scores(n):
        hh, qi, kb = blocks[n]
        qa = jnp.concatenate([q_ref[qi * tq:(qi + 1) * tq, hh * HEAD_DIM:(hh + 1) * HEAD_DIM], q_extra], axis=1)
        s_sc[n % nbuf] = _dot_nt(qa, kaug_sc[hh, kb * tk:(kb + 1) * tk, :])

    def softmax_pv(n):
        hh, qi, kb = blocks[n]
        first = kb == 0
        last = kb == (qi * tq) // tk
        s = s_sc[n % nbuf]
        if last:
            row = qi * tq + lax.broadcasted_iota(jnp.int32, s.shape, 0)
            col = kb * tk + lax.broadcasted_iota(jnp.int32, s.shape, 1)
            s = jnp.where(col <= row, s, NEG_INF)
        chunks = [s[:, c * LANES:(c + 1) * LANES] for c in range(nc)]
        bm = jnp.max(functools.reduce(jnp.maximum, chunks), axis=-1, keepdims=True)
        if first:
            m_new = jnp.broadcast_to(bm, (tq, LANES))
        else:
            m_prev = m_sc[hh, qi]
            m_new = jnp.maximum(m_prev, bm)
            alpha = jnp.exp2(m_prev - m_new)
        p = jnp.concatenate([jnp.exp2(c - m_new).astype(BF16) for c in chunks], axis=1)
        pv = _dot(p, vaug_sc[hh, kb * tk:(kb + 1) * tk, :])
        acc = pv if first else jnp.concatenate([alpha, alpha], axis=1) * acc_sc[hh, qi] + pv
        if last:
            o_ref[qi * tq:(qi + 1) * tq, hh * HEAD_DIM:(hh + 1) * HEAD_DIM] = (
                acc[:, :HEAD_DIM] / acc[:, HEAD_DIM:HEAD_DIM + 1])
        else:
            acc_sc[hh, qi] = acc
            m_sc[hh, qi] = m_new

    scores(0)
    for n in range(len(blocks)):
        if n + 1 < len(blocks):
            scores(n + 1)
        softmax_pv(n)


def _fox_prompt(q, k, v, ck, side, b, t, h):
    tq = _pick(t, (512, 256, 128))
    nq = t // tq
    nhead = _pick(h, (HEADS_PER_STEP, 1))
    hg = h // nhead
    blk = pl.BlockSpec((t, nhead * HEAD_DIM), lambda i, j: (i, j))
    side_specs, side_shapes = _side_specs(side, b * hg, lambda i, j: i * hg + j)
    o, *cast = pl.pallas_call(
        functools.partial(_fox_prompt_kernel, tq=tq, tk=tq, n_side=len(side)),
        grid=(b, hg),
        in_specs=[blk, blk, blk, pl.BlockSpec((None, nhead, 1, t), lambda i, j: (i, j, 0, 0))] + side_specs,
        out_specs=[blk] + side_specs,
        out_shape=[jax.ShapeDtypeStruct((b * t, h * HEAD_DIM), F32)] + side_shapes,
        scratch_shapes=[pltpu.VMEM((nhead, t, 2 * HEAD_DIM), BF16), pltpu.VMEM((nhead, t, 2 * HEAD_DIM), BF16),
                        pltpu.VMEM((nhead, nq, tq, LANES), F32), pltpu.VMEM((nhead, nq, tq, 2 * HEAD_DIM), F32),
                        pltpu.VMEM((SCORE_BUFFERS, tq, tq), F32)],
        compiler_params=_params("arbitrary", "arbitrary"),
        name="fox_prompt",
    )(q, k, v, ck, *side)
    return o, cast


def _tri2(tk):
    idx = jnp.arange(tk)
    tri = (idx[:, None] >= idx[None, :]).astype(BF16)
    return jnp.concatenate([tri, tri], axis=0)


SKIP_LOG2 = 152.0


def _sb_prompt_kernel(q_ref, k_ref, v_ref, tri2_ref, *rest, tb, n_side):
    side_in, (o_ref, *side_out) = rest[:n_side], rest[n_side:2 * n_side + 1]
    carry_sc, acc_sc, z_sc = rest[2 * n_side + 1:]
    _side_cast(list(side_in) + side_out)
    t = k_ref.shape[0]
    nq = t // tb
    nbuf = z_sc.shape[0]
    nhead = carry_sc.shape[0]
    tri2 = tri2_ref[...]
    nc = tb // LANES

    def head(hh):
        return slice(hh * HEAD_DIM, (hh + 1) * HEAD_DIM)

    def q_tile(hh, qi):
        return q_ref[qi * tb:(qi + 1) * tb, head(hh)]

    def weights(z, carry, diagonal):
        sp = _softplus2(z)
        if diagonal:
            row = lax.broadcasted_iota(jnp.int32, z.shape, 0)
            col = lax.broadcasted_iota(jnp.int32, z.shape, 1)
            valid = col < row
            sp = jnp.where(valid, sp, 0.0)
        incl = _dot(_split2(sp), tri2)
        arg = z - incl
        if carry is not None:
            arg = arg - jnp.concatenate([carry] * nc, axis=1)
        a = jnp.exp2(arg)
        if diagonal:
            a = jnp.where(valid, a, 0.0)
        tot = incl[:, :1]
        new_carry = jnp.broadcast_to(tot, (tb, LANES)) if carry is None else carry + tot
        return a.astype(BF16), new_carry

    blocks = [(hh, qi, kb) for qi in range(nq) for kb in (qi, qi - 1) if kb >= 0 for hh in range(nhead)]

    def scores(n):
        hh, qi, kb = blocks[n]
        z_sc[n % nbuf] = _dot_nt(q_tile(hh, qi), k_ref[kb * tb:(kb + 1) * tb, head(hh)])

    def near(n):
        hh, qi, kb = blocks[n]
        first = kb == qi
        a, carry = weights(z_sc[n % nbuf], None if first else carry_sc[hh, qi], first)
        pv = _dot(a, v_ref[kb * tb:(kb + 1) * tb, head(hh)])
        acc = pv if first else acc_sc[hh, qi] + pv
        if kb == qi - 1 or kb == 0:
            o_ref[qi * tb:(qi + 1) * tb, head(hh)] = acc
        if kb > 0:
            acc_sc[hh, qi] = acc
            carry_sc[hh, qi] = carry

    scores(0)
    for n in range(len(blocks)):
        if n + 1 < len(blocks):
            scores(n + 1)
        near(n)

    if nq <= 2:
        return
    tiles = [(hh, qi) for hh in range(nhead) for qi in range(2, nq)]
    lightest = functools.reduce(jnp.minimum, [carry_sc[hh, qi] for hh, qi in tiles])

    @pl.when(jnp.min(lightest) <= SKIP_LOG2)
    def _():
        for hh, qi in tiles:
            def more(kb):
                return jnp.logical_and(kb >= 0, jnp.min(carry_sc[hh, qi]) <= SKIP_LOG2)

            def far(kb):
                start = pl.multiple_of(kb * tb, tb)
                z = _dot_nt(q_tile(hh, qi), k_ref[pl.ds(start, tb), head(hh)])
                a, carry = weights(z, carry_sc[hh, qi], False)
                acc_sc[hh, qi] = acc_sc[hh, qi] + _dot(a, v_ref[pl.ds(start, tb), head(hh)])
                carry_sc[hh, qi] = carry
                return kb - 1

            lax.while_loop(more, far, qi - 2)
            o_ref[qi * tb:(qi + 1) * tb, head(hh)] = acc_sc[hh, qi]


def _sb_prompt(q, k, v, side, b, t, h):
    tb = _pick(t, (256, 128))
    nq = t // tb
    nhead = _pick(h, (HEADS_PER_STEP, 1))
    hg = h // nhead
    blk = pl.BlockSpec((t, nhead * HEAD_DIM), lambda i, j: (i, j))
    side_specs, side_shapes = _side_specs(side, b * hg, lambda i, j: i * hg + j)
    o, *cast = pl.pallas_call(
        functools.partial(_sb_prompt_kernel, tb=tb, n_side=len(side)),
        grid=(b, hg),
        in_specs=[blk, blk, blk, pl.BlockSpec((2 * tb, tb), lambda i, j: (0, 0))] + side_specs,
        out_specs=[blk] + side_specs,
        out_shape=[jax.ShapeDtypeStruct((b * t, h * HEAD_DIM), F32)] + side_shapes,
        scratch_shapes=[pltpu.VMEM((nhead, nq, tb, LANES), F32), pltpu.VMEM((nhead, nq, tb, HEAD_DIM), F32),
                        pltpu.VMEM((SCORE_BUFFERS, tb, tb), F32)],
        compiler_params=_params("arbitrary", "arbitrary"),
        name="sb_prompt",
    )(q, k, v, _tri2(tb), *side)
    return o, cast


def _head_rows(ref, h, nh, past):
    return ref[pl.ds(h, past, stride=nh), :].astype(BF16)


def _fox_sample_kernel(q_ref, kc_ref, vc_ref, kn_ref, vn_ref, ckc_ref, ckn_ref, o_ref, *, nh, past):
    ts = q_ref.shape[0]
    row = lax.broadcasted_iota(jnp.int32, (ts, ts), 0)
    col = lax.broadcasted_iota(jnp.int32, (ts, ts), 1)
    for h in range(nh):
        sl = slice(h * HEAD_DIM, (h + 1) * HEAD_DIM)
        q = q_ref[:, sl]
        s1 = _dot_nt(q, _head_rows(kc_ref, h, nh, past)) - ckc_ref[h:h + 1, :] * LOG2E
        s2 = _dot_nt(q, kn_ref[:, sl]) - ckn_ref[h:h + 1, :] * LOG2E
        s2 = jnp.where(col <= row, s2, NEG_INF)
        m = jnp.maximum(jnp.max(s1, axis=-1, keepdims=True), jnp.max(s2, axis=-1, keepdims=True))
        p1 = jnp.exp2(s1 - m)
        p2 = jnp.exp2(s2 - m)
        l = jnp.sum(p1, axis=-1, keepdims=True) + jnp.sum(p2, axis=-1, keepdims=True)
        acc = _dot(p1.astype(BF16), _head_rows(vc_ref, h, nh, past)) + _dot(p2.astype(BF16), vn_ref[:, sl])
        o_ref[:, sl] = acc / l


def _sb_sample_kernel(q_ref, kc_ref, vc_ref, kn_ref, vn_ref, tric_ref, trin_ref, o_ref, *, nh, past, tc):
    ts = q_ref.shape[0]
    row = lax.broadcasted_iota(jnp.int32, (ts, ts), 0)
    col = lax.broadcasted_iota(jnp.int32, (ts, ts), 1)
    valid = col < row
    tric = tric_ref[...]
    trin = trin_ref[...]
    nchunk = past // tc
    for h in range(nh):
        sl = slice(h * HEAD_DIM, (h + 1) * HEAD_DIM)
        q = q_ref[:, sl]
        vc = _head_rows(vc_ref, h, nh, past)
        z2 = _dot_nt(q, kn_ref[:, sl])
        incl2 = _dot(_split2(jnp.where(valid, _softplus2(z2), 0.0)), trin)
        a2 = jnp.where(valid, jnp.exp2(z2 - incl2), 0.0)
        acc = _dot(a2.astype(BF16), vn_ref[:, sl])
        carry = incl2[:, :1]
        z1 = _dot_nt(q, _head_rows(kc_ref, h, nh, past))
        sp1 = _softplus2(z1)
        stacked = jnp.concatenate([sp1[:, c * tc:(c + 1) * tc] for c in range(nchunk)], axis=0)
        incl = _dot(_split2(stacked), tric)
        for c in reversed(range(nchunk)):
            inc_c = incl[c * ts:(c + 1) * ts, :]
            a = jnp.exp2(z1[:, c * tc:(c + 1) * tc] - inc_c - carry)
            acc = acc + _dot(a.astype(BF16), vc[c * tc:(c + 1) * tc, :])
            carry = carry + inc_c[:, :1]
        o_ref[:, sl] = acc


def _sample_specs(ts, past, nh):
    new = pl.BlockSpec((ts, nh * HEAD_DIM), lambda i: (i, 0))
    cache = pl.BlockSpec((None, past * nh, HEAD_DIM), lambda i: (i, 0, 0))
    return new, cache


def _fox_sample(q, kc, vc, kn, vn, ckc, ckn, b, ts, past, h):
    new, cache = _sample_specs(ts, past, h)
    return pl.pallas_call(
        functools.partial(_fox_sample_kernel, nh=h, past=past),
        grid=(b,),
        in_specs=[new, cache, cache, new, new,
                  pl.BlockSpec((None, h, past), lambda i: (i, 0, 0)),
                  pl.BlockSpec((None, h, ts), lambda i: (i, 0, 0))],
        out_specs=new,
        out_shape=jax.ShapeDtypeStruct((b * ts, h * HEAD_DIM), F32),
        compiler_params=_params("parallel"),
        name="fox_sample",
    )(q, kc, vc, kn, vn, ckc, ckn)


def _sb_sample(q, kc, vc, kn, vn, b, ts, past, h):
    new, cache = _sample_specs(ts, past, h)
    tc = _pick(past, (256, 128))
    return pl.pallas_call(
        functools.partial(_sb_sample_kernel, nh=h, past=past, tc=tc),
        grid=(b,),
        in_specs=[new, cache, cache, new, new,
                  pl.BlockSpec((2 * tc, tc), lambda i: (0, 0)),
                  pl.BlockSpec((2 * ts, ts), lambda i: (0, 0))],
        out_specs=new,
        out_shape=jax.ShapeDtypeStruct((b * ts, h * HEAD_DIM), F32),
        compiler_params=_params("parallel"),
        name="sb_sample",
    )(q, kc, vc, kn, vn, _tri2(tc), _tri2(ts))


def _layer(x, b, t, w, caches):
    hf = w["b_forget"].shape[0]
    hs = w["h_sb"]
    q_scale = HEAD_DIM ** -0.5 * LOG2E
    big = (1024, 512, 256, 128)

    first = [(w["w_in_t"], hf * HEAD_DIM)] if caches is None else []
    hn, logf_pad, cast = _attn_norm(x, w["attn_norm"], w["w_fl"], w["b_fl"], first)
    if first:
        w["w_q_fox"] = cast[0]
    qkv = {}
    for grp in ("fox", "sb"):
        pending = w["to_cast_" + grp] if caches is None else []
        q, cast = _proj_q(hn, w["w_q_" + grp], q_scale,
                          [(w["w_in_t"], row0, count) for _, row0, count in pending], "proj_q_" + grp)
        for (key, _, _), arr in zip(pending, cast):
            w[key] = arr
        wk, wv = w["w_k_" + grp], w["w_v_" + grp]
        k32, k16 = _proj_kv(hn, wk, "proj_k_" + grp)
        v32, v16 = _proj_kv(hn, wv, "proj_v_" + grp)
        qkv[grp] = (q, k32, k16, v32, v16)

    qf, kf32, kf, vf32, vf = qkv["fox"]
    qs, ks32, ks, vs32, vs = qkv["sb"]
    zeros_init = jnp.zeros((b, 1, LANES), F32)
    if caches is None:
        ct, _ = _cumsum_t(logf_pad.reshape(b, t, LANES), zeros_init)
        o_fox, (w["w_gate"], w["w_up"]) = _fox_prompt(qf, kf, vf, ct[:, :hf, :].reshape(b, hf, 1, t),
                                                       [w["w_gate_f32"], w["w_up_f32"]], b, t, hf)
        o_sb, (w["w_down"], w["w_out"]) = _sb_prompt(qs, ks, vs, [w["w_down_f32"], w["w_out_f32"]], b, t, hs)
    else:
        c_fk, c_fv, c_fl, c_sk, c_sv = caches
        past = c_fk.shape[1]
        c_fl_pad = jnp.pad(c_fl.astype(F32), ((0, 0), (0, 0), (0, LANES - hf)))
        ct_c, last = _cumsum_t(c_fl_pad, zeros_init)
        ct_n, _ = _cumsum_t(logf_pad.reshape(b, t, LANES), last)
        o_fox = _fox_sample(qf, c_fk.reshape(b, past * hf, HEAD_DIM), c_fv.reshape(b, past * hf, HEAD_DIM), kf, vf,
                            ct_c[:, :hf, :], ct_n[:, :hf, :], b, t, past, hf)
        o_sb = _sb_sample(qs, c_sk.reshape(b, past * hs, HEAD_DIM), c_sv.reshape(b, past * hs, HEAD_DIM), ks, vs,
                          b, t, past, hs)

    a = _merge_norm(o_fox, o_sb, w["out_norm_fox"], w["out_norm_sb"])
    (x1,) = _matmul_call(_proj_resid_kernel, a, [w["w_out"]], [], [x], [F32], big, "proj_out")
    h2 = _rmsnorm(x1, w["ffn_norm"], BF16)
    (act,) = _matmul_call(_proj_swiglu_kernel, h2, [w["w_gate"], w["w_up"]], [], [], [BF16], (256, 128), "ffn_up",
                          tm_candidates=(2048, 1024, 512, 256, 128, 64, 32, 16, 8))
    (x2,) = _matmul_call(_proj_resid_kernel, act, [w["w_down"]], [], [x1], [F32], (512, 256, 128), "ffn_down",
                         tm_candidates=(512, 256, 128, 64, 32, 16, 8))
    return x2, kf32, vf32, logf_pad[:, :hf], ks32, vs32


def kernel(x_prompt, x_sample, cache_fox_k, cache_fox_v, cache_fox_logf, cache_sb_k, cache_sb_v, attn_norm, w_in,
           b_forget, out_norm_fox, out_norm_sb, w_out, ffn_norm, w_gate, w_up, w_down, final_norm):
    bp, tp, d = x_prompt.shape
    bs, ts, _ = x_sample.shape
    depth = w_in.shape[0]
    hf = cache_fox_k.shape[3]
    hs = cache_sb_k.shape[3]
    w_fox = hf * HEAD_DIM
    w_sb = hs * HEAD_DIM

    xp = x_prompt.reshape(bp * tp, d)
    xs = x_sample.reshape(bs * ts, d)
    outs_p, outs_s = [], []
    for l in range(depth):
        cuts = [0, w_fox, 2 * w_fox, 3 * w_fox, 3 * w_fox + hf, 3 * w_fox + hf + w_sb, 3 * w_fox + hf + 2 * w_sb,
                3 * w_fox + hf + 3 * w_sb]
        wl = jnp.swapaxes(w_in[l], 0, 1)
        w = {
            "attn_norm": attn_norm[l], "ffn_norm": ffn_norm[l],
            "out_norm_fox": out_norm_fox[l], "out_norm_sb": out_norm_sb[l],
            "h_sb": hs, "w_in_t": wl,
            "to_cast_fox": [("w_k_fox", cuts[1], w_fox), ("w_v_fox", cuts[2], w_fox), ("w_q_sb", cuts[4], w_sb)],
            "to_cast_sb": [("w_k_sb", cuts[5], w_sb), ("w_v_sb", cuts[6], w_sb)],
            "w_fl": jnp.pad(wl[cuts[3]:cuts[4], :], ((0, LANES - hf), (0, 0))).astype(BF16),
            "b_fl": jnp.pad(b_forget[l].astype(F32), (0, LANES - hf)).reshape(1, LANES),
            "b_forget": b_forget[l],
            "w_out_f32": w_out[l], "w_gate_f32": w_gate[l], "w_up_f32": w_up[l], "w_down_f32": w_down[l],
        }
        xp, *rest_p = _layer(xp, bp, tp, w, None)
        caches = (cache_fox_k[l], cache_fox_v[l], cache_fox_logf[l], cache_sb_k[l], cache_sb_v[l])
        xs, *rest_s = _layer(xs, bs, ts, w, caches)
        outs_p.append(rest_p)
        outs_s.append(rest_s)

    y_prompt = _rmsnorm(xp, final_norm, F32).reshape(bp, tp, d)
    y_sample = _rmsnorm(xs, final_norm, F32).reshape(bs, ts, d)

    def stack(outs, idx, shape):
        return jnp.stack([o[idx].reshape(shape) for o in outs])

    res = [y_prompt, y_sample]
    for outs, b, t in ((outs_p, bp, tp), (outs_s, bs, ts)):
        res += [stack(outs, 0, (b, t, hf, HEAD_DIM)), stack(outs, 1, (b, t, hf, HEAD_DIM)),
                stack(outs, 2, (b, t, hf)),
                stack(outs, 3, (b, t, hs, HEAD_DIM)), stack(outs, 4, (b, t, hs, HEAD_DIM))]
    return tuple(res)
```

```python
import functools
import math

import jax
import jax.numpy as jnp
from jax import lax
from jax.experimental import pallas as pl
from jax.experimental.pallas import tpu as pltpu

EPS = 1e-6
NEG_INF = -1e30
HEAD_DIM = 128
LANES = 128
SUBLANES = 8
LOG2E = math.log2(math.e)
VMEM_LIMIT_BYTES = 56 * 1024 * 1024
SCORE_BUFFERS = 3
HEADS_PER_STEP = 2
NORM_ROWS = (512, 256, 128, 64, 32, 16, 8)

F32 = jnp.float32
BF16 = jnp.bfloat16


def _params(*sem):
    return pltpu.CompilerParams(dimension_semantics=sem, vmem_limit_bytes=VMEM_LIMIT_BYTES)


def _pick(n, candidates):
    for c in candidates:
        if n % c == 0:
            return c
    return n


def _softplus(z):
    return jnp.maximum(z, 0.0) + jnp.log1p(jnp.exp(-jnp.abs(z)))


def _softplus2(z):
    return jnp.maximum(z, 0.0) + jnp.log2(1.0 + jnp.exp2(-jnp.abs(z)))


def _dot(a, b):
    return jnp.dot(a, b, preferred_element_type=F32)


def _dot_nt(a, b):
    return lax.dot_general(a, b, (((1,), (1,)), ((), ())), preferred_element_type=F32)


def _split2(x):
    hi = x.astype(BF16)
    lo = (x - hi.astype(F32)).astype(BF16)
    return jnp.concatenate([hi, lo], axis=1)


def _split3(x):
    hi = x.astype(BF16)
    r = x - hi.astype(F32)
    mid = r.astype(BF16)
    lo = (r - mid.astype(F32)).astype(BF16)
    return hi, mid, lo


def _rmsnorm_kernel(x_ref, g_ref, o_ref):
    x = x_ref[...]
    r = lax.rsqrt(jnp.mean(x * x, axis=-1, keepdims=True) + EPS)
    o_ref[...] = (x * r * g_ref[...]).astype(o_ref.dtype)


def _rmsnorm(x, g, out_dtype):
    m, d = x.shape
    tm = _pick(m, NORM_ROWS)
    return pl.pallas_call(
        _rmsnorm_kernel,
        grid=(m // tm,),
        in_specs=[pl.BlockSpec((tm, d), lambda i: (i, 0)),
                  pl.BlockSpec((1, d), lambda i: (0, 0))],
        out_specs=pl.BlockSpec((tm, d), lambda i: (i, 0)),
        out_shape=jax.ShapeDtypeStruct((m, d), out_dtype),
        compiler_params=_params("parallel"),
        name="rmsnorm",
    )(x, g.reshape(1, d).astype(F32))


def _merge_norm_kernel(a_ref, b_ref, ga_ref, gb_ref, o_ref):
    wa = a_ref.shape[1]
    for ref, g_ref, lo in ((a_ref, ga_ref, 0), (b_ref, gb_ref, wa)):
        x = ref[...]
        r = lax.rsqrt(jnp.mean(x * x, axis=-1, keepdims=True) + EPS)
        o_ref[:, lo:lo + x.shape[1]] = (x * r * g_ref[...]).astype(o_ref.dtype)


def _merge_norm(o_fox, o_sb, g_fox, g_sb):
    m, wa = o_fox.shape
    wb = o_sb.shape[1]
    tm = _pick(m, NORM_ROWS)
    return pl.pallas_call(
        _merge_norm_kernel,
        grid=(m // tm,),
        in_specs=[pl.BlockSpec((tm, wa), lambda i: (i, 0)),
                  pl.BlockSpec((tm, wb), lambda i: (i, 0)),
                  pl.BlockSpec((1, wa), lambda i: (0, 0)),
                  pl.BlockSpec((1, wb), lambda i: (0, 0))],
        out_specs=pl.BlockSpec((tm, wa + wb), lambda i: (i, 0)),
        out_shape=jax.ShapeDtypeStruct((m, wa + wb), BF16),
        compiler_params=_params("parallel"),
        name="merge_norm",
    )(o_fox, o_sb, g_fox.reshape(1, wa).astype(F32), g_sb.reshape(1, wb).astype(F32))


def _proj_q_kernel(x_ref, w_ref, *rest, scale, n_side):
    side_in, (o_ref, *side_out) = rest[:n_side], rest[n_side:]
    _side_cast(list(side_in) + side_out)
    o_ref[...] = (_dot_nt(x_ref[...], w_ref[...]) * scale).astype(o_ref.dtype)


def _proj_q(x, w, scale, side, name):
    m, k = x.shape
    n = w.shape[0]
    tm = _pick(m, (1024, 512, 256, 128, 64, 32, 16, 8))
    tn = _pick(n, (1024, 512, 256, 128))
    nj = n // tn
    steps = (m // tm) * nj
    aligned = [row0 % SUBLANES == 0 for _, row0, _ in side]
    plain = [arr[row0:row0 + count].astype(BF16) for (arr, row0, count), ok in zip(side, aligned) if not ok]
    kept = [s for s, ok in zip(side, aligned) if ok]
    side_in, side_out, side_shapes = [], [], []
    for arr, row0, count in kept:
        rows = _side_rows(count, steps)
        last = count // rows - 1
        block = lambda i, j, last=last: jnp.minimum(i * nj + j, last)
        side_in.append(pl.BlockSpec((pl.Element(rows), pl.Element(arr.shape[1])),
                                    lambda i, j, row0=row0, rows=rows, block=block:
                                    (SUBLANES * (row0 // SUBLANES + block(i, j) * (rows // SUBLANES)), 0)))
        side_out.append(pl.BlockSpec((rows, arr.shape[1]), lambda i, j, block=block: (block(i, j), 0)))
        side_shapes.append(jax.ShapeDtypeStruct((count, arr.shape[1]), BF16))
    q, *cast = pl.pallas_call(
        functools.partial(_proj_q_kernel, scale=scale, n_side=len(kept)),
        grid=(m // tm, nj),
        in_specs=[pl.BlockSpec((tm, k), lambda i, j: (i, 0)), pl.BlockSpec((tn, k), lambda i, j: (j, 0))] + side_in,
        out_specs=[pl.BlockSpec((tm, tn), lambda i, j: (i, j))] + side_out,
        out_shape=[jax.ShapeDtypeStruct((m, n), BF16)] + side_shapes,
        compiler_params=_params("arbitrary", "arbitrary"),
        name=name,
    )(x, w, *[arr for arr, _, _ in kept])
    cast, plain = iter(cast), iter(plain)
    return q, [next(cast) if ok else next(plain) for ok in aligned]


def _proj_kv_kernel(x_ref, w_ref, o32_ref, o16_ref, *, nh):
    acc = _dot_nt(x_ref[...], w_ref[...])
    o16_ref[...] = acc.astype(o16_ref.dtype)
    tm = x_ref.shape[0]
    for h in range(nh):
        o32_ref[pl.ds(h, tm, stride=nh), :] = acc[:, h * HEAD_DIM:(h + 1) * HEAD_DIM]


def _proj_kv(x, w, name):
    m, k = x.shape
    n = w.shape[0]
    nh = n // HEAD_DIM
    tm = _pick(m, (512, 256, 128, 64, 32, 16, 8))
    return pl.pallas_call(
        functools.partial(_proj_kv_kernel, nh=nh),
        grid=(m // tm,),
        in_specs=[pl.BlockSpec((tm, k), lambda i: (i, 0)),
                  pl.BlockSpec((n, k), lambda i: (0, 0))],
        out_specs=[pl.BlockSpec((tm * nh, HEAD_DIM), lambda i: (i, 0)),
                   pl.BlockSpec((tm, n), lambda i: (i, 0))],
        out_shape=[jax.ShapeDtypeStruct((m * nh, HEAD_DIM), F32), jax.ShapeDtypeStruct((m, n), BF16)],
        compiler_params=_params("parallel"),
        name=name,
    )(x, w)


def _attn_norm_kernel(x_ref, g_ref, wfl_ref, bfl_ref, *rest, n_side):
    side_in, (h_ref, logf_ref, *side_out) = rest[:n_side], rest[n_side:]
    _side_cast(list(side_in) + side_out)
    x = x_ref[...]
    r = lax.rsqrt(jnp.mean(x * x, axis=-1, keepdims=True) + EPS)
    h = (x * r * g_ref[...]).astype(h_ref.dtype)
    h_ref[...] = h
    logf_ref[...] = -_softplus(-(_dot_nt(h, wfl_ref[...]) + bfl_ref[...]))


def _attn_norm(x, g, w_fl, b_fl, side):
    m, d = x.shape
    tm = _pick(m, NORM_ROWS)
    steps = m // tm
    side_specs, side_shapes = _side_specs(side, steps, lambda i: i)
    row = pl.BlockSpec((tm, d), lambda i: (i, 0))
    h, logf, *cast = pl.pallas_call(
        functools.partial(_attn_norm_kernel, n_side=len(side)),
        grid=(steps,),
        in_specs=[row, pl.BlockSpec((1, d), lambda i: (0, 0)), pl.BlockSpec((LANES, d), lambda i: (0, 0)),
                  pl.BlockSpec((1, LANES), lambda i: (0, 0))] + side_specs,
        out_specs=[row, pl.BlockSpec((tm, LANES), lambda i: (i, 0))] + side_specs,
        out_shape=[jax.ShapeDtypeStruct((m, d), BF16), jax.ShapeDtypeStruct((m, LANES), F32)] + side_shapes,
        compiler_params=_params("arbitrary"),
        name="attn_norm",
    )(x, g.reshape(1, d).astype(F32), w_fl, b_fl, *[s[0] if isinstance(s, tuple) else s for s in side])
    return h, logf, cast


def _proj_resid_kernel(x_ref, w_ref, r_ref, o_ref):
    o_ref[...] = r_ref[...] + _dot(x_ref[...], w_ref[...])


def _proj_swiglu_kernel(x_ref, wg_ref, wu_ref, o_ref):
    x = x_ref[...]
    g = _dot(x, wg_ref[...])
    u = _dot(x, wu_ref[...])
    o_ref[...] = (g * jax.nn.sigmoid(g) * u).astype(o_ref.dtype)


def _matmul_call(kernel, x, weights, extra_row_inputs, extra_tile_inputs, out_dtypes, tn_candidates, name,
                 tm_candidates=(1024, 512, 256, 128, 64, 32, 16, 8)):
    m, k = x.shape
    n = weights[0].shape[1]
    tm = _pick(m, tm_candidates)
    tn = _pick(n, tn_candidates)
    tile = pl.BlockSpec((tm, tn), lambda i, j: (i, j))
    in_specs = [pl.BlockSpec((tm, k), lambda i, j: (i, 0))]
    in_specs += [pl.BlockSpec((k, tn), lambda i, j: (0, j)) for _ in weights]
    in_specs += [pl.BlockSpec((1, tn), lambda i, j: (0, j)) for _ in extra_row_inputs]
    in_specs += [tile for _ in extra_tile_inputs]
    outs = pl.pallas_call(
        kernel,
        grid=(m // tm, n // tn),
        in_specs=in_specs,
        out_specs=[tile for _ in out_dtypes],
        out_shape=[jax.ShapeDtypeStruct((m, n), dt) for dt in out_dtypes],
        compiler_params=_params("parallel", "arbitrary"),
        name=name,
    )(x, *weights, *extra_row_inputs, *extra_tile_inputs)
    return outs


def _cumsum_kernel(x_ref, init_ref, tril_ref, ct_ref, last_ref, carry_sc):
    t = pl.program_id(1)

    @pl.when(t == 0)
    def _():
        carry_sc[...] = init_ref[...]

    parts = jnp.concatenate(_split3(x_ref[...]), axis=1)
    s = _dot(tril_ref[...], parts)
    cum = s[:, :LANES] + s[:, LANES:2 * LANES] + s[:, 2 * LANES:] + carry_sc[...]
    carry_sc[...] = cum[-1:, :]
    ct_ref[...] = cum.T
    last_ref[...] = cum[-1:, :]


def _cumsum_t(x, init):
    b, t, _ = x.shape
    tt = _pick(t, (512, 256, 128))
    idx = jnp.arange(tt)
    tril = (idx[:, None] >= idx[None, :]).astype(BF16)
    return pl.pallas_call(
        _cumsum_kernel,
        grid=(b, t // tt),
        in_specs=[pl.BlockSpec((None, tt, LANES), lambda i, j: (i, j, 0)),
                  pl.BlockSpec((None, 1, LANES), lambda i, j: (i, 0, 0)),
                  pl.BlockSpec((tt, tt), lambda i, j: (0, 0))],
        out_specs=[pl.BlockSpec((None, LANES, tt), lambda i, j: (i, 0, j)),
                   pl.BlockSpec((None, 1, LANES), lambda i, j: (i, 0, 0))],
        out_shape=[jax.ShapeDtypeStruct((b, LANES, t), F32),
                   jax.ShapeDtypeStruct((b, 1, LANES), F32)],
        scratch_shapes=[pltpu.VMEM((1, LANES), F32)],
        compiler_params=_params("parallel", "arbitrary"),
        name="cumsum_t",
    )(x, init, tril)


def _side_rows(k, steps):
    for rows in range(16, k + 1, 16):
        if k % rows == 0 and k // rows <= steps:
            return rows
    return None


def _side_specs(weights, steps, step_index):
    specs, shapes = [], []
    for w in weights:
        w, count = w if isinstance(w, tuple) else (w, w.shape[0])
        rows = _side_rows(count, steps)
        last = count // rows - 1
        specs.append(pl.BlockSpec((rows, w.shape[1]),
                                  lambda *g, last=last: (jnp.minimum(step_index(*g), last), 0)))
        shapes.append(jax.ShapeDtypeStruct((count, w.shape[1]), BF16))
    return specs, shapes


def _side_cast(refs):
    n = len(refs) // 2
    for w_ref, o_ref in zip(refs[:n], refs[n:]):
        o_ref[...] = w_ref[...].astype(o_ref.dtype)


def _fox_prompt_kernel(q_ref, k_ref, v_ref, ck_ref, *rest, tq, tk, n_side):
    side_in, (o_ref, *side_out) = rest[:n_side], rest[n_side:2 * n_side + 1]
    kaug_sc, vaug_sc, m_sc, acc_sc, s_sc = rest[2 * n_side + 1:]
    _side_cast(list(side_in) + side_out)
    t = k_ref.shape[0]
    nq = t // tq
    nbuf = s_sc.shape[0]
    nhead = kaug_sc.shape[0]
    rows = lax.broadcasted_iota(jnp.int32, (HEAD_DIM, t), 0)
    lane = lax.broadcasted_iota(jnp.int32, (t, HEAD_DIM), 1)
    for hh in range(nhead):
        hi, mid, lo = _split3(ck_ref[hh] * LOG2E)
        parts = jnp.where(rows == 0, hi.astype(F32),
                          jnp.where(rows == 1, mid.astype(F32), jnp.where(rows == 2, lo.astype(F32), 0.0)))
        kaug_sc[hh, :, :HEAD_DIM] = k_ref[:, hh * HEAD_DIM:(hh + 1) * HEAD_DIM]
        kaug_sc[hh, :, HEAD_DIM:] = parts.T.astype(BF16)
        vaug_sc[hh, :, :HEAD_DIM] = v_ref[:, hh * HEAD_DIM:(hh + 1) * HEAD_DIM]
        vaug_sc[hh, :, HEAD_DIM:] = jnp.where(lane == 0, 1.0, 0.0).astype(BF16)

    lane_q = lax.broadcasted_iota(jnp.int32, (tq, HEAD_DIM), 1)
    q_extra = jnp.where(lane_q < 3, -1.0, 0.0).astype(BF16)
    nc = tk // LANES
    blocks = [(hh, qi, kb) for qi in range(nq) for kb in range((qi * tq) // tk + 1) for hh in range(nhead)]

    def scores(n):
        hh, qi, kb = blocks[n]
        qa = jnp.concatenate([q_ref[qi * tq:(qi + 1) * tq, hh * HEAD_DIM:(hh + 1) * HEAD_DIM], q_extra], axis=1)
        s_sc[n % nbuf] = _dot_nt(qa, kaug_sc[hh, kb * tk:(kb + 1) * tk, :])

    def softmax_pv(n):
        hh, qi, kb = blocks[n]
        first = kb == 0
        last = kb == (qi * tq) // tk
        s = s_sc[n % nbuf]
        if last:
            row = qi * tq + lax.broadcasted_iota(jnp.int32, s.shape, 0)
            col = kb * tk + lax.broadcasted_iota(jnp.int32, s.shape, 1)
            s = jnp.where(col <= row, s, NEG_INF)
        chunks = [s[:, c * LANES:(c + 1) * LANES] for c in range(nc)]
        bm = jnp.max(functools.reduce(jnp.maximum, chunks), axis=-1, keepdims=True)
        if first:
            m_new = jnp.broadcast_to(bm, (tq, LANES))
        else:
            m_prev = m_sc[hh, qi]
            m_new = jnp.maximum(m_prev, bm)
            alpha = jnp.exp2(m_prev - m_new)
        p = jnp.concatenate([jnp.exp2(c - m_new).astype(BF16) for c in chunks], axis=1)
        pv = _dot(p, vaug_sc[hh, kb * tk:(kb + 1) * tk, :])
        acc = pv if first else jnp.concatenate([alpha, alpha], axis=1) * acc_sc[hh, qi] + pv
        if last:
            o_ref[qi * tq:(qi + 1) * tq, hh * HEAD_DIM:(hh + 1) * HEAD_DIM] = (
                acc[:, :HEAD_DIM] / acc[:, HEAD_DIM:HEAD_DIM + 1])
        else:
            acc_sc[hh, qi] = acc
            m_sc[hh, qi] = m_new

    scores(0)
    for n in range(len(blocks)):
        if n + 1 < len(blocks):
            scores(n + 1)
        softmax_pv(n)


def _fox_prompt(q, k, v, ck, side, b, t, h):
    tq = _pick(t, (512, 256, 128))
    nq = t // tq
    nhead = _pick(h, (HEADS_PER_STEP, 1))
    hg = h // nhead
    blk = pl.BlockSpec((t, nhead * HEAD_DIM), lambda i, j: (i, j))
    side_specs, side_shapes = _side_specs(side, b * hg, lambda i, j: i * hg + j)
    o, *cast = pl.pallas_call(
        functools.partial(_fox_prompt_kernel, tq=tq, tk=tq, n_side=len(side)),
        grid=(b, hg),
        in_specs=[blk, blk, blk, pl.BlockSpec((None, nhead, 1, t), lambda i, j: (i, j, 0, 0))] + side_specs,
        out_specs=[blk] + side_specs,
        out_shape=[jax.ShapeDtypeStruct((b * t, h * HEAD_DIM), F32)] + side_shapes,
        scratch_shapes=[pltpu.VMEM((nhead, t, 2 * HEAD_DIM), BF16), pltpu.VMEM((nhead, t, 2 * HEAD_DIM), BF16),
                        pltpu.VMEM((nhead, nq, tq, LANES), F32), pltpu.VMEM((nhead, nq, tq, 2 * HEAD_DIM), F32),
                        pltpu.VMEM((SCORE_BUFFERS, tq, tq), F32)],
        compiler_params=_params("arbitrary", "arbitrary"),
        name="fox_prompt",
    )(q, k, v, ck, *side)
    return o, cast


def _tri2(tk):
    idx = jnp.arange(tk)
    tri = (idx[:, None] >= idx[None, :]).astype(BF16)
    return jnp.concatenate([tri, tri], axis=0)


SKIP_LOG2 = 152.0


def _sb_prompt_kernel(q_ref, k_ref, v_ref, tri2_ref, *rest, tb, n_side):
    side_in, (o_ref, *side_out) = rest[:n_side], rest[n_side:2 * n_side + 1]
    carry_sc, acc_sc, z_sc = rest[2 * n_side + 1:]
    _side_cast(list(side_in) + side_out)
    t = k_ref.shape[0]
    nq = t // tb
    nbuf = z_sc.shape[0]
    nhead = carry_sc.shape[0]
    tri2 = tri2_ref[...]
    nc = tb // LANES

    def head(hh):
        return slice(hh * HEAD_DIM, (hh + 1) * HEAD_DIM)

    def q_tile(hh, qi):
        return q_ref[qi * tb:(qi + 1) * tb, head(hh)]

    def weights(z, carry, diagonal):
        sp = _softplus2(z)
        if diagonal:
            row = lax.broadcasted_iota(jnp.int32, z.shape, 0)
            col = lax.broadcasted_iota(jnp.int32, z.shape, 1)
            valid = col < row
            sp = jnp.where(valid, sp, 0.0)
        incl = _dot(_split2(sp), tri2)
        arg = z - incl
        if carry is not None:
            arg = arg - jnp.concatenate([carry] * nc, axis=1)
        a = jnp.exp2(arg)
        if diagonal:
            a = jnp.where(valid, a, 0.0)
        tot = incl[:, :1]
        new_carry = jnp.broadcast_to(tot, (tb, LANES)) if carry is None else carry + tot
        return a.astype(BF16), new_carry

    blocks = [(hh, qi, kb) for qi in range(nq) for kb in (qi, qi - 1) if kb >= 0 for hh in range(nhead)]

    def scores(n):
        hh, qi, kb = blocks[n]
        z_sc[n % nbuf] = _dot_nt(q_tile(hh, qi), k_ref[kb * tb:(kb + 1) * tb, head(hh)])

    def near(n):
        hh, qi, kb = blocks[n]
        first = kb == qi
        a, carry = weights(z_sc[n % nbuf], None if first else carry_sc[hh, qi], first)
        pv = _dot(a, v_ref[kb * tb:(kb + 1) * tb, head(hh)])
        acc = pv if first else acc_sc[hh, qi] + pv
        if kb == qi - 1 or kb == 0:
            o_ref[qi * tb:(qi + 1) * tb, head(hh)] = acc
        if kb > 0:
            acc_sc[hh, qi] = acc
            carry_sc[hh, qi] = carry

    scores(0)
    for n in range(len(blocks)):
        if n + 1 < len(blocks):
            scores(n + 1)
        near(n)

    if nq <= 2:
        return
    tiles = [(hh, qi) for hh in range(nhead) for qi in range(2, nq)]
    lightest = functools.reduce(jnp.minimum, [carry_sc[hh, qi] for hh, qi in tiles])

    @pl.when(jnp.min(lightest) <= SKIP_LOG2)
    def _():
        for hh, qi in tiles:
            def more(kb):
                return jnp.logical_and(kb >= 0, jnp.min(carry_sc[hh, qi]) <= SKIP_LOG2)

            def far(kb):
                start = pl.multiple_of(kb * tb, tb)
                z = _dot_nt(q_tile(hh, qi), k_ref[pl.ds(start, tb), head(hh)])
                a, carry = weights(z, carry_sc[hh, qi], False)
                acc_sc[hh, qi] = acc_sc[hh, qi] + _dot(a, v_ref[pl.ds(start, tb), head(hh)])
                carry_sc[hh, qi] = carry
                return kb - 1

            lax.while_loop(more, far, qi - 2)
            o_ref[qi * tb:(qi + 1) * tb, head(hh)] = acc_sc[hh, qi]


def _sb_prompt(q, k, v, side, b, t, h):
    tb = _pick(t, (256, 128))
    nq = t // tb
    nhead = _pick(h, (HEADS_PER_STEP, 1))
    hg = h // nhead
    blk = pl.BlockSpec((t, nhead * HEAD_DIM), lambda i, j: (i, j))
    side_specs, side_shapes = _side_specs(side, b * hg, lambda i, j: i * hg + j)
    o, *cast = pl.pallas_call(
        functools.partial(_sb_prompt_kernel, tb=tb, n_side=len(side)),
        grid=(b, hg),
        in_specs=[blk, blk, blk, pl.BlockSpec((2 * tb, tb), lambda i, j: (0, 0))] + side_specs,
        out_specs=[blk] + side_specs,
        out_shape=[jax.ShapeDtypeStruct((b * t, h * HEAD_DIM), F32)] + side_shapes,
        scratch_shapes=[pltpu.VMEM((nhead, nq, tb, LANES), F32), pltpu.VMEM((nhead, nq, tb, HEAD_DIM), F32),
                        pltpu.VMEM((SCORE_BUFFERS, tb, tb), F32)],
        compiler_params=_params("arbitrary", "arbitrary"),
        name="sb_prompt",
    )(q, k, v, _tri2(tb), *side)
    return o, cast


def _head_rows(ref, h, nh, past):
    return ref[pl.ds(h, past, stride=nh), :].astype(BF16)


def _fox_sample_kernel(q_ref, kc_ref, vc_ref, kn_ref, vn_ref, ckc_ref, ckn_ref, o_ref, *, nh, past):
    ts = q_ref.shape[0]
    row = lax.broadcasted_iota(jnp.int32, (ts, ts), 0)
    col = lax.broadcasted_iota(jnp.int32, (ts, ts), 1)
    for h in range(nh):
        sl = slice(h * HEAD_DIM, (h + 1) * HEAD_DIM)
        q = q_ref[:, sl]
        s1 = _dot_nt(q, _head_rows(kc_ref, h, nh, past)) - ckc_ref[h:h + 1, :] * LOG2E
        s2 = _dot_nt(q, kn_ref[:, sl]) - ckn_ref[h:h + 1, :] * LOG2E
        s2 = jnp.where(col <= row, s2, NEG_INF)
        m = jnp.maximum(jnp.max(s1, axis=-1, keepdims=True), jnp.max(s2, axis=-1, keepdims=True))
        p1 = jnp.exp2(s1 - m)
        p2 = jnp.exp2(s2 - m)
        l = jnp.sum(p1, axis=-1, keepdims=True) + jnp.sum(p2, axis=-1, keepdims=True)
        acc = _dot(p1.astype(BF16), _head_rows(vc_ref, h, nh, past)) + _dot(p2.astype(BF16), vn_ref[:, sl])
        o_ref[:, sl] = acc / l


def _sb_sample_kernel(q_ref, kc_ref, vc_ref, kn_ref, vn_ref, tric_ref, trin_ref, o_ref, *, nh, past, tc):
    ts = q_ref.shape[0]
    row = lax.broadcasted_iota(jnp.int32, (ts, ts), 0)
    col = lax.broadcasted_iota(jnp.int32, (ts, ts), 1)
    valid = col < row
    tric = tric_ref[...]
    trin = trin_ref[...]
    nchunk = past // tc
    for h in range(nh):
        sl = slice(h * HEAD_DIM, (h + 1) * HEAD_DIM)
        q = q_ref[:, sl]
        vc = _head_rows(vc_ref, h, nh, past)
        z2 = _dot_nt(q, kn_ref[:, sl])
        incl2 = _dot(_split2(jnp.where(valid, _softplus2(z2), 0.0)), trin)
        a2 = jnp.where(valid, jnp.exp2(z2 - incl2), 0.0)
        acc = _dot(a2.astype(BF16), vn_ref[:, sl])
        carry = incl2[:, :1]
        z1 = _dot_nt(q, _head_rows(kc_ref, h, nh, past))
        sp1 = _softplus2(z1)
        stacked = jnp.concatenate([sp1[:, c * tc:(c + 1) * tc] for c in range(nchunk)], axis=0)
        incl = _dot(_split2(stacked), tric)
        for c in reversed(range(nchunk)):
            inc_c = incl[c * ts:(c + 1) * ts, :]
            a = jnp.exp2(z1[:, c * tc:(c + 1) * tc] - inc_c - carry)
            acc = acc + _dot(a.astype(BF16), vc[c * tc:(c + 1) * tc, :])
            carry = carry + inc_c[:, :1]
        o_ref[:, sl] = acc


def _sample_specs(ts, past, nh):
    new = pl.BlockSpec((ts, nh * HEAD_DIM), lambda i: (i, 0))
    cache = pl.BlockSpec((None, past * nh, HEAD_DIM), lambda i: (i, 0, 0))
    return new, cache


def _fox_sample(q, kc, vc, kn, vn, ckc, ckn, b, ts, past, h):
    new, cache = _sample_specs(ts, past, h)
    return pl.pallas_call(
        functools.partial(_fox_sample_kernel, nh=h, past=past),
        grid=(b,),
        in_specs=[new, cache, cache, new, new,
                  pl.BlockSpec((None, h, past), lambda i: (i, 0, 0)),
                  pl.BlockSpec((None, h, ts), lambda i: (i, 0, 0))],
        out_specs=new,
        out_shape=jax.ShapeDtypeStruct((b * ts, h * HEAD_DIM), F32),
        compiler_params=_params("parallel"),
        name="fox_sample",
    )(q, kc, vc, kn, vn, ckc, ckn)


def _sb_sample(q, kc, vc, kn, vn, b, ts, past, h):
    new, cache = _sample_specs(ts, past, h)
    tc = _pick(past, (256, 128))
    return pl.pallas_call(
        functools.partial(_sb_sample_kernel, nh=h, past=past, tc=tc),
        grid=(b,),
        in_specs=[new, cache, cache, new, new,
                  pl.BlockSpec((2 * tc, tc), lambda i: (0, 0)),
                  pl.BlockSpec((2 * ts, ts), lambda i: (0, 0))],
        out_specs=new,
        out_shape=jax.ShapeDtypeStruct((b * ts, h * HEAD_DIM), F32),
        compiler_params=_params("parallel"),
        name="sb_sample",
    )(q, kc, vc, kn, vn, _tri2(tc), _tri2(ts))


def _layer(x, b, t, w, caches):
    hf = w["b_forget"].shape[0]
    hs = w["h_sb"]
    q_scale = HEAD_DIM ** -0.5 * LOG2E
    big = (1024, 512, 256, 128)

    first = [(w["w_in_t"], hf * HEAD_DIM)] if caches is None else []
    hn, logf_pad, cast = _attn_norm(x, w["attn_norm"], w["w_fl"], w["b_fl"], first)
    if first:
        w["w_q_fox"] = cast[0]
    qkv = {}
    for grp in ("fox", "sb"):
        pending = w["to_cast_" + grp] if caches is None else []
        q, cast = _proj_q(hn, w["w_q_" + grp], q_scale,
                          [(w["w_in_t"], row0, count) for _, row0, count in pending], "proj_q_" + grp)
        for (key, _, _), arr in zip(pending, cast):
            w[key] = arr
        wk, wv = w["w_k_" + grp], w["w_v_" + grp]
        k32, k16 = _proj_kv(hn, wk, "proj_k_" + grp)
        v32, v16 = _proj_kv(hn, wv, "proj_v_" + grp)
        qkv[grp] = (q, k32, k16, v32, v16)

    qf, kf32, kf, vf32, vf = qkv["fox"]
    qs, ks32, ks, vs32, vs = qkv["sb"]
    zeros_init = jnp.zeros((b, 1, LANES), F32)
    if caches is None:
        ct, _ = _cumsum_t(logf_pad.reshape(b, t, LANES), zeros_init)
        o_fox, (w["w_gate"], w["w_up"]) = _fox_prompt(qf, kf, vf, ct[:, :hf, :].reshape(b, hf, 1, t),
                                                       [w["w_gate_f32"], w["w_up_f32"]], b, t, hf)
        o_sb, (w["w_down"], w["w_out"]) = _sb_prompt(qs, ks, vs, [w["w_down_f32"], w["w_out_f32"]], b, t, hs)
    else:
        c_fk, c_fv, c_fl, c_sk, c_sv = caches
        past = c_fk.shape[1]
        c_fl_pad = jnp.pad(c_fl.astype(F32), ((0, 0), (0, 0), (0, LANES - hf)))
        ct_c, last = _cumsum_t(c_fl_pad, zeros_init)
        ct_n, _ = _cumsum_t(logf_pad.reshape(b, t, LANES), last)
        o_fox = _fox_sample(qf, c_fk.reshape(b, past * hf, HEAD_DIM), c_fv.reshape(b, past * hf, HEAD_DIM), kf, vf,
                            ct_c[:, :hf, :], ct_n[:, :hf, :], b, t, past, hf)
        o_sb = _sb_sample(qs, c_sk.reshape(b, past * hs, HEAD_DIM), c_sv.reshape(b, past * hs, HEAD_DIM), ks, vs,
                          b, t, past, hs)

    a = _merge_norm(o_fox, o_sb, w["out_norm_fox"], w["out_norm_sb"])
    (x1,) = _matmul_call(_proj_resid_kernel, a, [w["w_out"]], [], [x], [F32], big, "proj_out")
    h2 = _rmsnorm(x1, w["ffn_norm"], BF16)
    (act,) = _matmul_call(_proj_swiglu_kernel, h2, [w["w_gate"], w["w_up"]], [], [], [BF16], (256, 128), "ffn_up",
                          tm_candidates=(2048, 1024, 512, 256, 128, 64, 32, 16, 8))
    (x2,) = _matmul_call(_proj_resid_kernel, act, [w["w_down"]], [], [x1], [F32], (512, 256, 128), "ffn_down",
                         tm_candidates=(512, 256, 128, 64, 32, 16, 8))
    return x2, kf32, vf32, logf_pad[:, :hf], ks32, vs32


def kernel(x_prompt, x_sample, cache_fox_k, cache_fox_v, cache_fox_logf, cache_sb_k, cache_sb_v, attn_norm, w_in,
           b_forget, out_norm_fox, out_norm_sb, w_out, ffn_norm, w_gate, w_up, w_down, final_norm):
    bp, tp, d = x_prompt.shape
    bs, ts, _ = x_sample.shape
    depth = w_in.shape[0]
    hf = cache_fox_k.shape[3]
    hs = cache_sb_k.shape[3]
    w_fox = hf * HEAD_DIM
    w_sb = hs * HEAD_DIM

    xp = x_prompt.reshape(bp * tp, d)
    xs = x_sample.reshape(bs * ts, d)
    outs_p, outs_s = [], []
    for l in range(depth):
        cuts = [0, w_fox, 2 * w_fox, 3 * w_fox, 3 * w_fox + hf, 3 * w_fox + hf + w_sb, 3 * w_fox + hf + 2 * w_sb,
                3 * w_fox + hf + 3 * w_sb]
        wl = jnp.swapaxes(w_in[l], 0, 1)
        w = {
            "attn_norm": attn_norm[l], "ffn_norm": ffn_norm[l],
            "out_norm_fox": out_norm_fox[l], "out_norm_sb": out_norm_sb[l],
            "h_sb": hs, "w_in_t": wl,
            "to_cast_fox": [("w_k_fox", cuts[1], w_fox), ("w_v_fox", cuts[2], w_fox), ("w_q_sb", cuts[4], w_sb)],
            "to_cast_sb": [("w_k_sb", cuts[5], w_sb), ("w_v_sb", cuts[6], w_sb)],
            "w_fl": jnp.pad(wl[cuts[3]:cuts[4], :], ((0, LANES - hf), (0, 0))).astype(BF16),
            "b_fl": jnp.pad(b_forget[l].astype(F32), (0, LANES - hf)).reshape(1, LANES),
            "b_forget": b_forget[l],
            "w_out_f32": w_out[l], "w_gate_f32": w_gate[l], "w_up_f32": w_up[l], "w_down_f32": w_down[l],
        }
        xp, *rest_p = _layer(xp, bp, tp, w, None)
        caches = (cache_fox_k[l], cache_fox_v[l], cache_fox_logf[l], cache_sb_k[l], cache_sb_v[l])
        xs, *rest_s = _layer(xs, bs, ts, w, caches)
        outs_p.append(rest_p)
        outs_s.append(rest_s)

    y_prompt = _rmsnorm(xp, final_norm, F32).reshape(bp, tp, d)
    y_sample = _rmsnorm(xs, final_norm, F32).reshape(bs, ts, d)

    def stack(outs, idx, shape):
        return jnp.stack([o[idx].reshape(shape) for o in outs])

    res = [y_prompt, y_sample]
    for outs, b, t in ((outs_p, bp, tp), (outs_s, bs, ts)):
        res += [stack(outs, 0, (b, t, hf, HEAD_DIM)), stack(outs, 1, (b, t, hf, HEAD_DIM)),
                stack(outs, 2, (b, t, hf)),
                stack(outs, 3, (b, t, hs, HEAD_DIM)), stack(outs, 4, (b, t, hs, HEAD_DIM))]
    return tuple(res)
```

```python
import functools
import math

import jax
import jax.numpy as jnp
from jax import lax
from jax.experimental import pallas as pl
from jax.experimental.pallas import tpu as pltpu

EPS = 1e-6
NEG_INF = -1e30
HEAD_DIM = 128
LANES = 128
SUBLANES = 8
LOG2E = math.log2(math.e)
VMEM_LIMIT_BYTES = 56 * 1024 * 1024
SCORE_BUFFERS = 3
HEADS_PER_STEP = 2
NORM_ROWS = (512, 256, 128, 64, 32, 16, 8)

F32 = jnp.float32
BF16 = jnp.bfloat16


def _params(*sem):
    return pltpu.CompilerParams(dimension_semantics=sem, vmem_limit_bytes=VMEM_LIMIT_BYTES)


def _pick(n, candidates):
    for c in candidates:
        if n % c == 0:
            return c
    return n


def _softplus(z):
    return jnp.maximum(z, 0.0) + jnp.log1p(jnp.exp(-jnp.abs(z)))


def _softplus2(z):
    return jnp.maximum(z, 0.0) + jnp.log2(1.0 + jnp.exp2(-jnp.abs(z)))


def _dot(a, b):
    return jnp.dot(a, b, preferred_element_type=F32)


def _dot_nt(a, b):
    return lax.dot_general(a, b, (((1,), (1,)), ((), ())), preferred_element_type=F32)


def _split2(x):
    hi = x.astype(BF16)
    lo = (x - hi.astype(F32)).astype(BF16)
    return jnp.concatenate([hi, lo], axis=1)


def _split3(x):
    hi = x.astype(BF16)
    r = x - hi.astype(F32)
    mid = r.astype(BF16)
    lo = (r - mid.astype(F32)).astype(BF16)
    return hi, mid, lo


def _rmsnorm_kernel(x_ref, g_ref, o_ref):
    x = x_ref[...]
    r = lax.rsqrt(jnp.mean(x * x, axis=-1, keepdims=True) + EPS)
    o_ref[...] = (x * r * g_ref[...]).astype(o_ref.dtype)


def _rmsnorm(x, g, out_dtype):
    m, d = x.shape
    tm = _pick(m, NORM_ROWS)
    return pl.pallas_call(
        _rmsnorm_kernel,
        grid=(m // tm,),
        in_specs=[pl.BlockSpec((tm, d), lambda i: (i, 0)),
                  pl.BlockSpec((1, d), lambda i: (0, 0))],
        out_specs=pl.BlockSpec((tm, d), lambda i: (i, 0)),
        out_shape=jax.ShapeDtypeStruct((m, d), out_dtype),
        compiler_params=_params("parallel"),
        name="rmsnorm",
    )(x, g.reshape(1, d).astype(F32))


def _merge_norm_kernel(a_ref, b_ref, ga_ref, gb_ref, o_ref):
    wa = a_ref.shape[1]
    for ref, g_ref, lo in ((a_ref, ga_ref, 0), (b_ref, gb_ref, wa)):
        x = ref[...]
        r = lax.rsqrt(jnp.mean(x * x, axis=-1, keepdims=True) + EPS)
        o_ref[:, lo:lo + x.shape[1]] = (x * r * g_ref[...]).astype(o_ref.dtype)


def _merge_norm(o_fox, o_sb, g_fox, g_sb):
    m, wa = o_fox.shape
    wb = o_sb.shape[1]
    tm = _pick(m, NORM_ROWS)
    return pl.pallas_call(
        _merge_norm_kernel,
        grid=(m // tm,),
        in_specs=[pl.BlockSpec((tm, wa), lambda i: (i, 0)),
                  pl.BlockSpec((tm, wb), lambda i: (i, 0)),
                  pl.BlockSpec((1, wa), lambda i: (0, 0)),
                  pl.BlockSpec((1, wb), lambda i: (0, 0))],
        out_specs=pl.BlockSpec((tm, wa + wb), lambda i: (i, 0)),
        out_shape=jax.ShapeDtypeStruct((m, wa + wb), BF16),
        compiler_params=_params("parallel"),
        name="merge_norm",
    )(o_fox, o_sb, g_fox.reshape(1, wa).astype(F32), g_sb.reshape(1, wb).astype(F32))


def _proj_q_kernel(x_ref, w_ref, *rest, scale, n_side):
    side_in, (o_ref, *side_out) = rest[:n_side], rest[n_side:]
    _side_cast(list(side_in) + side_out)
    o_ref[...] = (_dot_nt(x_ref[...], w_ref[...]) * scale).astype(o_ref.dtype)


def _proj_q(x, w, scale, side, name):
    m, k = x.shape
    n = w.shape[0]
    tm = _pick(m, (1024, 512, 256, 128, 64, 32, 16, 8))
    tn = _pick(n, (1024, 512, 256, 128))
    nj = n // tn
    steps = (m // tm) * nj
    aligned = [row0 % SUBLANES == 0 for _, row0, _ in side]
    plain = [arr[row0:row0 + count].astype(BF16) for (arr, row0, count), ok in zip(side, aligned) if not ok]
    kept = [s for s, ok in zip(side, aligned) if ok]
    side_in, side_out, side_shapes = [], [], []
    for arr, row0, count in kept:
        rows = _side_rows(count, steps)
        last = count // rows - 1
        block = lambda i, j, last=last: jnp.minimum(i * nj + j, last)
        side_in.append(pl.BlockSpec((pl.Element(rows), pl.Element(arr.shape[1])),
                                    lambda i, j, row0=row0, rows=rows, block=block:
                                    (SUBLANES * (row0 // SUBLANES + block(i, j) * (rows // SUBLANES)), 0)))
        side_out.append(pl.BlockSpec((rows, arr.shape[1]), lambda i, j, block=block: (block(i, j), 0)))
        side_shapes.append(jax.ShapeDtypeStruct((count, arr.shape[1]), BF16))
    q, *cast = pl.pallas_call(
        functools.partial(_proj_q_kernel, scale=scale, n_side=len(kept)),
        grid=(m // tm, nj),
        in_specs=[pl.BlockSpec((tm, k), lambda i, j: (i, 0)), pl.BlockSpec((tn, k), lambda i, j: (j, 0))] + side_in,
        out_specs=[pl.BlockSpec((tm, tn), lambda i, j: (i, j))] + side_out,
        out_shape=[jax.ShapeDtypeStruct((m, n), BF16)] + side_shapes,
        compiler_params=_params("arbitrary", "arbitrary"),
        name=name,
    )(x, w, *[arr for arr, _, _ in kept])
    cast, plain = iter(cast), iter(plain)
    return q, [next(cast) if ok else next(plain) for ok in aligned]


def _proj_kv_kernel(x_ref, w_ref, o32_ref, o16_ref, *, nh):
    acc = _dot_nt(x_ref[...], w_ref[...])
    o16_ref[...] = acc.astype(o16_ref.dtype)
    tm = x_ref.shape[0]
    for h in range(nh):
        o32_ref[pl.ds(h, tm, stride=nh), :] = acc[:, h * HEAD_DIM:(h + 1) * HEAD_DIM]


def _proj_kv(x, w, name):
    m, k = x.shape
    n = w.shape[0]
    nh = n // HEAD_DIM
    tm = _pick(m, (512, 256, 128, 64, 32, 16, 8))
    return pl.pallas_call(
        functools.partial(_proj_kv_kernel, nh=nh),
        grid=(m // tm,),
        in_specs=[pl.BlockSpec((tm, k), lambda i: (i, 0)),
                  pl.BlockSpec((n, k), lambda i: (0, 0))],
        out_specs=[pl.BlockSpec((tm * nh, HEAD_DIM), lambda i: (i, 0)),
                   pl.BlockSpec((tm, n), lambda i: (i, 0))],
        out_shape=[jax.ShapeDtypeStruct((m * nh, HEAD_DIM), F32), jax.ShapeDtypeStruct((m, n), BF16)],
        compiler_params=_params("parallel"),
        name=name,
    )(x, w)


def _attn_norm_kernel(x_ref, g_ref, wfl_ref, bfl_ref, *rest, n_side):
    side_in, (h_ref, logf_ref, *side_out) = rest[:n_side], rest[n_side:]
    _side_cast(list(side_in) + side_out)
    x = x_ref[...]
    r = lax.rsqrt(jnp.mean(x * x, axis=-1, keepdims=True) + EPS)
    h = (x * r * g_ref[...]).astype(h_ref.dtype)
    h_ref[...] = h
    logf_ref[...] = -_softplus(-(_dot_nt(h, wfl_ref[...]) + bfl_ref[...]))


def _attn_norm(x, g, w_fl, b_fl, side):
    m, d = x.shape
    tm = _pick(m, NORM_ROWS)
    steps = m // tm
    side_specs, side_shapes = _side_specs(side, steps, lambda i: i)
    row = pl.BlockSpec((tm, d), lambda i: (i, 0))
    h, logf, *cast = pl.pallas_call(
        functools.partial(_attn_norm_kernel, n_side=len(side)),
        grid=(steps,),
        in_specs=[row, pl.BlockSpec((1, d), lambda i: (0, 0)), pl.BlockSpec((LANES, d), lambda i: (0, 0)),
                  pl.BlockSpec((1, LANES), lambda i: (0, 0))] + side_specs,
        out_specs=[row, pl.BlockSpec((tm, LANES), lambda i: (i, 0))] + side_specs,
        out_shape=[jax.ShapeDtypeStruct((m, d), BF16), jax.ShapeDtypeStruct((m, LANES), F32)] + side_shapes,
        compiler_params=_params("arbitrary"),
        name="attn_norm",
    )(x, g.reshape(1, d).astype(F32), w_fl, b_fl, *[s[0] if isinstance(s, tuple) else s for s in side])
    return h, logf, cast


def _proj_resid_kernel(x_ref, w_ref, r_ref, o_ref):
    o_ref[...] = r_ref[...] + _dot(x_ref[...], w_ref[...])


def _proj_swiglu_kernel(x_ref, wg_ref, wu_ref, o_ref):
    x = x_ref[...]
    g = _dot(x, wg_ref[...])
    u = _dot(x, wu_ref[...])
    o_ref[...] = (g * jax.nn.sigmoid(g) * u).astype(o_ref.dtype)


def _matmul_call(kernel, x, weights, extra_row_inputs, extra_tile_inputs, out_dtypes, tn_candidates, name,
                 tm_candidates=(1024, 512, 256, 128, 64, 32, 16, 8)):
    m, k = x.shape
    n = weights[0].shape[1]
    tm = _pick(m, tm_candidates)
    tn = _pick(n, tn_candidates)
    tile = pl.BlockSpec((tm, tn), lambda i, j: (i, j))
    in_specs = [pl.BlockSpec((tm, k), lambda i, j: (i, 0))]
    in_specs += [pl.BlockSpec((k, tn), lambda i, j: (0, j)) for _ in weights]
    in_specs += [pl.BlockSpec((1, tn), lambda i, j: (0, j)) for _ in extra_row_inputs]
    in_specs += [tile for _ in extra_tile_inputs]
    outs = pl.pallas_call(
        kernel,
        grid=(m // tm, n // tn),
        in_specs=in_specs,
        out_specs=[tile for _ in out_dtypes],
        out_shape=[jax.ShapeDtypeStruct((m, n), dt) for dt in out_dtypes],
        compiler_params=_params("parallel", "arbitrary"),
        name=name,
    )(x, *weights, *extra_row_inputs, *extra_tile_inputs)
    return outs


def _cumsum_kernel(x_ref, init_ref, tril_ref, ct_ref, last_ref, carry_sc):
    t = pl.program_id(1)

    @pl.when(t == 0)
    def _():
        carry_sc[...] = init_ref[...]

    parts = jnp.concatenate(_split3(x_ref[...]), axis=1)
    s = _dot(tril_ref[...], parts)
    cum = s[:, :LANES] + s[:, LANES:2 * LANES] + s[:, 2 * LANES:] + carry_sc[...]
    carry_sc[...] = cum[-1:, :]
    ct_ref[...] = cum.T
    last_ref[...] = cum[-1:, :]


def _cumsum_t(x, init):
    b, t, _ = x.shape
    tt = _pick(t, (512, 256, 128))
    idx = jnp.arange(tt)
    tril = (idx[:, None] >= idx[None, :]).astype(BF16)
    return pl.pallas_call(
        _cumsum_kernel,
        grid=(b, t // tt),
        in_specs=[pl.BlockSpec((None, tt, LANES), lambda i, j: (i, j, 0)),
                  pl.BlockSpec((None, 1, LANES), lambda i, j: (i, 0, 0)),
                  pl.BlockSpec((tt, tt), lambda i, j: (0, 0))],
        out_specs=[pl.BlockSpec((None, LANES, tt), lambda i, j: (i, 0, j)),
                   pl.BlockSpec((None, 1, LANES), lambda i, j: (i, 0, 0))],
        out_shape=[jax.ShapeDtypeStruct((b, LANES, t), F32),
                   jax.ShapeDtypeStruct((b, 1, LANES), F32)],
        scratch_shapes=[pltpu.VMEM((1, LANES), F32)],
        compiler_params=_params("parallel", "arbitrary"),
        name="cumsum_t",
    )(x, init, tril)


def _side_rows(k, steps):
    for rows in range(16, k + 1, 16):
        if k % rows == 0 and k // rows <= steps:
            return rows
    return None


def _side_specs(weights, steps, step_index):
    specs, shapes = [], []
    for w in weights:
        w, count = w if isinstance(w, tuple) else (w, w.shape[0])
        rows = _side_rows(count, steps)
        last = count // rows - 1
        specs.append(pl.BlockSpec((rows, w.shape[1]),
                                  lambda *g, last=last: (jnp.minimum(step_index(*g), last), 0)))
        shapes.append(jax.ShapeDtypeStruct((count, w.shape[1]), BF16))
    return specs, shapes


def _side_cast(refs):
    n = len(refs) // 2
    for w_ref, o_ref in zip(refs[:n], refs[n:]):
        o_ref[...] = w_ref[...].astype(o_ref.dtype)


def _fox_prompt_kernel(q_ref, k_ref, v_ref, ck_ref, *rest, tq, tk, n_side):
    side_in, (o_ref, *side_out) = rest[:n_side], rest[n_side:2 * n_side + 1]
    kaug_sc, vaug_sc, m_sc, acc_sc, s_sc = rest[2 * n_side + 1:]
    _side_cast(list(side_in) + side_out)
    t = k_ref.shape[0]
    nq = t // tq
    nbuf = s_sc.shape[0]
    nhead = kaug_sc.shape[0]
    rows = lax.broadcasted_iota(jnp.int32, (HEAD_DIM, t), 0)
    lane = lax.broadcasted_iota(jnp.int32, (t, HEAD_DIM), 1)
    for hh in range(nhead):
        hi, mid, lo = _split3(ck_ref[hh] * LOG2E)
        parts = jnp.where(rows == 0, hi.astype(F32),
                          jnp.where(rows == 1, mid.astype(F32), jnp.where(rows == 2, lo.astype(F32), 0.0)))
        kaug_sc[hh, :, :HEAD_DIM] = k_ref[:, hh * HEAD_DIM:(hh + 1) * HEAD_DIM]
        kaug_sc[hh, :, HEAD_DIM:] = parts.T.astype(BF16)
        vaug_sc[hh, :, :HEAD_DIM] = v_ref[:, hh * HEAD_DIM:(hh + 1) * HEAD_DIM]
        vaug_sc[hh, :, HEAD_DIM:] = jnp.where(lane == 0, 1.0, 0.0).astype(BF16)

    lane_q = lax.broadcasted_iota(jnp.int32, (tq, HEAD_DIM), 1)
    q_extra = jnp.where(lane_q < 3, -1.0, 0.0).astype(BF16)
    nc = tk // LANES
    blocks = [(hh, qi, kb) for qi in range(nq) for kb in range((qi * tq) // tk + 1) for hh in range(nhead)]

    def scores(n):
        hh, qi, kb = blocks[n]
        qa = jnp.concatenate([q_ref[qi * tq:(qi + 1) * tq, hh * HEAD_DIM:(hh + 1) * HEAD_DIM], q_extra], axis=1)
        s_sc[n % nbuf] = _dot_nt(qa, kaug_sc[hh, kb * tk:(kb + 1) * tk, :])

    def softmax_pv(n):
        hh, qi, kb = blocks[n]
        first = kb == 0
        last = kb == (qi * tq) // tk
        s = s_sc[n % nbuf]
        if last:
            row = qi * tq + lax.broadcasted_iota(jnp.int32, s.shape, 0)
            col = kb * tk + lax.broadcasted_iota(jnp.int32, s.shape, 1)
            s = jnp.where(col <= row, s, NEG_INF)
        chunks = [s[:, c * LANES:(c + 1) * LANES] for c in range(nc)]
        bm = jnp.max(functools.reduce(jnp.maximum, chunks), axis=-1, keepdims=True)
        if first:
            m_new = jnp.broadcast_to(bm, (tq, LANES))
        else:
            m_prev = m_sc[hh, qi]
            m_new = jnp.maximum(m_prev, bm)
            alpha = jnp.exp2(m_prev - m_new)
        p = jnp.concatenate([jnp.exp2(c - m_new).astype(BF16) for c in chunks], axis=1)
        pv = _dot(p, vaug_sc[hh, kb * tk:(kb + 1) * tk, :])
        acc = pv if first else jnp.concatenate([alpha, alpha], axis=1) * acc_sc[hh, qi] + pv
        if last:
            o_ref[qi * tq:(qi + 1) * tq, hh * HEAD_DIM:(hh + 1) * HEAD_DIM] = (
                acc[:, :HEAD_DIM] / acc[:, HEAD_DIM:HEAD_DIM + 1])
        else:
            acc_sc[hh, qi] = acc
            m_sc[hh, qi] = m_new

    scores(0)
    for n in range(len(blocks)):
        if n + 1 < len(blocks):
            scores(n + 1)
        softmax_pv(n)


def _fox_prompt(q, k, v, ck, side, b, t, h):
    tq = _pick(t, (512, 256, 128))
    nq = t // tq
    nhead = _pick(h, (HEADS_PER_STEP, 1))
    hg = h // nhead
    blk = pl.BlockSpec((t, nhead * HEAD_DIM), lambda i, j: (i, j))
    side_specs, side_shapes = _side_specs(side, b * hg, lambda i, j: i * hg + j)
    o, *cast = pl.pallas_call(
        functools.partial(_fox_prompt_kernel, tq=tq, tk=tq, n_side=len(side)),
        grid=(b, hg),
        in_specs=[blk, blk, blk, pl.BlockSpec((None, nhead, 1, t), lambda i, j: (i, j, 0, 0))] + side_specs,
        out_specs=[blk] + side_specs,
        out_shape=[jax.ShapeDtypeStruct((b * t, h * HEAD_DIM), F32)] + side_shapes,
        scratch_shapes=[pltpu.VMEM((nhead, t, 2 * HEAD_DIM), BF16), pltpu.VMEM((nhead, t, 2 * HEAD_DIM), BF16),
                        pltpu.VMEM((nhead, nq, tq, LANES), F32), pltpu.VMEM((nhead, nq, tq, 2 * HEAD_DIM), F32),
                        pltpu.VMEM((SCORE_BUFFERS, tq, tq), F32)],
        compiler_params=_params("arbitrary", "arbitrary"),
        name="fox_prompt",
    )(q, k, v, ck, *side)
    return o, cast


def _tri2(tk):
    idx = jnp.arange(tk)
    tri = (idx[:, None] >= idx[None, :]).astype(BF16)
    return jnp.concatenate([tri, tri], axis=0)


SKIP_LOG2 = 152.0


def _sb_prompt_kernel(q_ref, k_ref, v_ref, tri2_ref, *rest, tb, n_side):
    side_in, (o_ref, *side_out) = rest[:n_side], rest[n_side:2 * n_side + 1]
    carry_sc, acc_sc, z_sc = rest[2 * n_side + 1:]
    _side_cast(list(side_in) + side_out)
    t = k_ref.shape[0]
    nq = t // tb
    nbuf = z_sc.shape[0]
    nhead = carry_sc.shape[0]
    tri2 = tri2_ref[...]
    nc = tb // LANES

    def head(hh):
        return slice(hh * HEAD_DIM, (hh + 1) * HEAD_DIM)

    def q_tile(hh, qi):
        return q_ref[qi * tb:(qi + 1) * tb, head(hh)]

    def weights(z, carry, diagonal):
        sp = _softplus2(z)
        if diagonal:
            row = lax.broadcasted_iota(jnp.int32, z.shape, 0)
            col = lax.broadcasted_iota(jnp.int32, z.shape, 1)
            valid = col < row
            sp = jnp.where(valid, sp, 0.0)
        incl = _dot(_split2(sp), tri2)
        arg = z - incl
        if carry is not None:
            arg = arg - jnp.concatenate([carry] * nc, axis=1)
        a = jnp.exp2(arg)
        if diagonal:
            a = jnp.where(valid, a, 0.0)
        tot = incl[:, :1]
        new_carry = jnp.broadcast_to(tot, (tb, LANES)) if carry is None else carry + tot
        return a.astype(BF16), new_carry

    blocks = [(hh, qi, kb) for qi in range(nq) for kb in (qi, qi - 1) if kb >= 0 for hh in range(nhead)]

    def scores(n):
        hh, qi, kb = blocks[n]
        z_sc[n % nbuf] = _dot_nt(q_tile(hh, qi), k_ref[kb * tb:(kb + 1) * tb, head(hh)])

    def near(n):
        hh, qi, kb = blocks[n]
        first = kb == qi
        a, carry = weights(z_sc[n % nbuf], None if first else carry_sc[hh, qi], first)
        pv = _dot(a, v_ref[kb * tb:(kb + 1) * tb, head(hh)])
        acc = pv if first else acc_sc[hh, qi] + pv
        if kb == qi - 1 or kb == 0:
            o_ref[qi * tb:(qi + 1) * tb, head(hh)] = acc
        if kb > 0:
            acc_sc[hh, qi] = acc
            carry_sc[hh, qi] = carry

    scores(0)
    for n in range(len(blocks)):
        if n + 1 < len(blocks):
            scores(n + 1)
        near(n)

    if nq <= 2:
        return
    tiles = [(hh, qi) for hh in range(nhead) for qi in range(2, nq)]
    lightest = functools.reduce(jnp.minimum, [carry_sc[hh, qi] for hh, qi in tiles])

    @pl.when(jnp.min(lightest) <= SKIP_LOG2)
    def _():
        for hh, qi in tiles:
            def more(kb):
                return jnp.logical_and(kb >= 0, jnp.min(carry_sc[hh, qi]) <= SKIP_LOG2)

            def far(kb):
                start = pl.multiple_of(kb * tb, tb)
                z = _dot_nt(q_tile(hh, qi), k_ref[pl.ds(start, tb), head(hh)])
                a, carry = weights(z, carry_sc[hh, qi], False)
                acc_sc[hh, qi] = acc_sc[hh, qi] + _dot(a, v_ref[pl.ds(start, tb), head(hh)])
                carry_sc[hh, qi] = carry
                return kb - 1

            lax.while_loop(more, far, qi - 2)
            o_ref[qi * tb:(qi + 1) * tb, head(hh)] = acc_sc[hh, qi]


def _sb_prompt(q, k, v, side, b, t, h):
    tb = _pick(t, (256, 128))
    nq = t // tb
    nhead = _pick(h, (HEADS_PER_STEP, 1))
    hg = h // nhead
    blk = pl.BlockSpec((t, nhead * HEAD_DIM), lambda i, j: (i, j))
    side_specs, side_shapes = _side_specs(side, b * hg, lambda i, j: i * hg + j)
    o, *cast = pl.pallas_call(
        functools.partial(_sb_prompt_kernel, tb=tb, n_side=len(side)),
        grid=(b, hg),
        in_specs=[blk, blk, blk, pl.BlockSpec((2 * tb, tb), lambda i, j: (0, 0))] + side_specs,
        out_specs=[blk] + side_specs,
        out_shape=[jax.ShapeDtypeStruct((b * t, h * HEAD_DIM), F32)] + side_shapes,
        scratch_shapes=[pltpu.VMEM((nhead, nq, tb, LANES), F32), pltpu.VMEM((nhead, nq, tb, HEAD_DIM), F32),
                        pltpu.VMEM((SCORE_BUFFERS, tb, tb), F32)],
        compiler_params=_params("arbitrary", "arbitrary"),
        name="sb_prompt",
    )(q, k, v, _tri2(tb), *side)
    return o, cast


class _HeadFetcher:
    def __init__(self, kc_hbm, vc_hbm, kbuf, vbuf, sem, nh):
        self.srcs, self.bufs, self.sem, self.nh = (kc_hbm, vc_hbm), (kbuf, vbuf), sem, nh
        self.b = pl.program_id(0)

    def _copies(self, h):
        slot = h % 2
        return [pltpu.make_async_copy(src.at[self.b, :, h, :], buf.at[slot], self.sem.at[i, slot])
                for i, (src, buf) in enumerate(zip(self.srcs, self.bufs))]

    def start(self, h):
        for c in self._copies(h):
            c.start()

    def get(self, h):
        if h + 1 < self.nh:
            self.start(h + 1)
        for c in self._copies(h):
            c.wait()
        return tuple(buf[h % 2].astype(BF16) for buf in self.bufs)


def _fox_sample_kernel(q_ref, kc_hbm, vc_hbm, kn_ref, vn_ref, ckc_ref, ckn_ref, o_ref, kbuf, vbuf, sem, *, nh):
    ts = q_ref.shape[0]
    row = lax.broadcasted_iota(jnp.int32, (ts, ts), 0)
    col = lax.broadcasted_iota(jnp.int32, (ts, ts), 1)
    fetch = _HeadFetcher(kc_hbm, vc_hbm, kbuf, vbuf, sem, nh)
    fetch.start(0)
    for h in range(nh):
        sl = slice(h * HEAD_DIM, (h + 1) * HEAD_DIM)
        q = q_ref[:, sl]
        kc, vc = fetch.get(h)
        s1 = _dot_nt(q, kc) - ckc_ref[h:h + 1, :] * LOG2E
        s2 = _dot_nt(q, kn_ref[:, sl]) - ckn_ref[h:h + 1, :] * LOG2E
        s2 = jnp.where(col <= row, s2, NEG_INF)
        m = jnp.maximum(jnp.max(s1, axis=-1, keepdims=True), jnp.max(s2, axis=-1, keepdims=True))
        p1 = jnp.exp2(s1 - m)
        p2 = jnp.exp2(s2 - m)
        l = jnp.sum(p1, axis=-1, keepdims=True) + jnp.sum(p2, axis=-1, keepdims=True)
        acc = _dot(p1.astype(BF16), vc) + _dot(p2.astype(BF16), vn_ref[:, sl])
        o_ref[:, sl] = acc / l


def _sb_sample_kernel(q_ref, kc_hbm, vc_hbm, kn_ref, vn_ref, tric_ref, trin_ref, o_ref, kbuf, vbuf, sem, *, nh, tc):
    ts = q_ref.shape[0]
    past = kbuf.shape[1]
    row = lax.broadcasted_iota(jnp.int32, (ts, ts), 0)
    col = lax.broadcasted_iota(jnp.int32, (ts, ts), 1)
    valid = col < row
    tric = tric_ref[...]
    trin = trin_ref[...]
    nchunk = past // tc
    fetch = _HeadFetcher(kc_hbm, vc_hbm, kbuf, vbuf, sem, nh)
    fetch.start(0)
    for h in range(nh):
        sl = slice(h * HEAD_DIM, (h + 1) * HEAD_DIM)
        q = q_ref[:, sl]
        kc, vc = fetch.get(h)
        z2 = _dot_nt(q, kn_ref[:, sl])
        incl2 = _dot(_split2(jnp.where(valid, _softplus2(z2), 0.0)), trin)
        a2 = jnp.where(valid, jnp.exp2(z2 - incl2), 0.0)
        acc = _dot(a2.astype(BF16), vn_ref[:, sl])
        carry = incl2[:, :1]
        z1 = _dot_nt(q, kc)
        sp1 = _softplus2(z1)
        stacked = jnp.concatenate([sp1[:, c * tc:(c + 1) * tc] for c in range(nchunk)], axis=0)
        incl = _dot(_split2(stacked), tric)
        for c in reversed(range(nchunk)):
            inc_c = incl[c * ts:(c + 1) * ts, :]
            a = jnp.exp2(z1[:, c * tc:(c + 1) * tc] - inc_c - carry)
            acc = acc + _dot(a.astype(BF16), vc[c * tc:(c + 1) * tc, :])
            carry = carry + inc_c[:, :1]
        o_ref[:, sl] = acc


def _sample_specs(ts, past, nh):
    new = pl.BlockSpec((ts, nh * HEAD_DIM), lambda i: (i, 0))
    cache = pl.BlockSpec(memory_space=pl.ANY)
    scratch = [pltpu.VMEM((2, past, HEAD_DIM), F32), pltpu.VMEM((2, past, HEAD_DIM), F32),
               pltpu.SemaphoreType.DMA((2, 2))]
    return new, cache, scratch


def _fox_sample(q, kc, vc, kn, vn, ckc, ckn, b, ts, past, h):
    new, cache, scratch = _sample_specs(ts, past, h)
    return pl.pallas_call(
        functools.partial(_fox_sample_kernel, nh=h),
        grid=(b,),
        in_specs=[new, cache, cache, new, new,
                  pl.BlockSpec((None, h, past), lambda i: (i, 0, 0)),
                  pl.BlockSpec((None, h, ts), lambda i: (i, 0, 0))],
        out_specs=new,
        out_shape=jax.ShapeDtypeStruct((b * ts, h * HEAD_DIM), F32),
        scratch_shapes=scratch,
        compiler_params=_params("arbitrary"),
        name="fox_sample",
    )(q, kc, vc, kn, vn, ckc, ckn)


def _sb_sample(q, kc, vc, kn, vn, b, ts, past, h):
    new, cache, scratch = _sample_specs(ts, past, h)
    tc = _pick(past, (256, 128))
    return pl.pallas_call(
        functools.partial(_sb_sample_kernel, nh=h, tc=tc),
        grid=(b,),
        in_specs=[new, cache, cache, new, new,
                  pl.BlockSpec((2 * tc, tc), lambda i: (0, 0)),
                  pl.BlockSpec((2 * ts, ts), lambda i: (0, 0))],
        out_specs=new,
        out_shape=jax.ShapeDtypeStruct((b * ts, h * HEAD_DIM), F32),
        scratch_shapes=scratch,
        compiler_params=_params("arbitrary"),
        name="sb_sample",
    )(q, kc, vc, kn, vn, _tri2(tc), _tri2(ts))


def _layer(x, b, t, w, caches):
    hf = w["b_forget"].shape[0]
    hs = w["h_sb"]
    q_scale = HEAD_DIM ** -0.5 * LOG2E
    big = (1024, 512, 256, 128)

    first = [(w["w_in_t"], hf * HEAD_DIM)] if caches is None else []
    hn, logf_pad, cast = _attn_norm(x, w["attn_norm"], w["w_fl"], w["b_fl"], first)
    if first:
        w["w_q_fox"] = cast[0]
    qkv = {}
    for grp in ("fox", "sb"):
        pending = w["to_cast_" + grp] if caches is None else []
        q, cast = _proj_q(hn, w["w_q_" + grp], q_scale,
                          [(w["w_in_t"], row0, count) for _, row0, count in pending], "proj_q_" + grp)
        for (key, _, _), arr in zip(pending, cast):
            w[key] = arr
        wk, wv = w["w_k_" + grp], w["w_v_" + grp]
        k32, k16 = _proj_kv(hn, wk, "proj_k_" + grp)
        v32, v16 = _proj_kv(hn, wv, "proj_v_" + grp)
        qkv[grp] = (q, k32, k16, v32, v16)

    qf, kf32, kf, vf32, vf = qkv["fox"]
    qs, ks32, ks, vs32, vs = qkv["sb"]
    zeros_init = jnp.zeros((b, 1, LANES), F32)
    if caches is None:
        ct, _ = _cumsum_t(logf_pad.reshape(b, t, LANES), zeros_init)
        o_fox, (w["w_gate"], w["w_up"]) = _fox_prompt(qf, kf, vf, ct[:, :hf, :].reshape(b, hf, 1, t),
                                                       [w["w_gate_f32"], w["w_up_f32"]], b, t, hf)
        o_sb, (w["w_down"], w["w_out"]) = _sb_prompt(qs, ks, vs, [w["w_down_f32"], w["w_out_f32"]], b, t, hs)
    else:
        c_fk, c_fv, c_fl, c_sk, c_sv = caches
        past = c_fk.shape[1]
        c_fl_pad = jnp.pad(c_fl.astype(F32), ((0, 0), (0, 0), (0, LANES - hf)))
        ct_c, last = _cumsum_t(c_fl_pad, zeros_init)
        ct_n, _ = _cumsum_t(logf_pad.reshape(b, t, LANES), last)
        o_fox = _fox_sample(qf, c_fk, c_fv, kf, vf, ct_c[:, :hf, :], ct_n[:, :hf, :], b, t, past, hf)
        o_sb = _sb_sample(qs, c_sk, c_sv, ks, vs, b, t, past, hs)

    a = _merge_norm(o_fox, o_sb, w["out_norm_fox"], w["out_norm_sb"])
    (x1,) = _matmul_call(_proj_resid_kernel, a, [w["w_out"]], [], [x], [F32], big, "proj_out")
    h2 = _rmsnorm(x1, w["ffn_norm"], BF16)
    (act,) = _matmul_call(_proj_swiglu_kernel, h2, [w["w_gate"], w["w_up"]], [], [], [BF16], (256, 128), "ffn_up",
                          tm_candidates=(2048, 1024, 512, 256, 128, 64, 32, 16, 8))
    (x2,) = _matmul_call(_proj_resid_kernel, act, [w["w_down"]], [], [x1], [F32], (512, 256, 128), "ffn_down",
                         tm_candidates=(512, 256, 128, 64, 32, 16, 8))
    return x2, kf32, vf32, logf_pad[:, :hf], ks32, vs32


def kernel(x_prompt, x_sample, cache_fox_k, cache_fox_v, cache_fox_logf, cache_sb_k, cache_sb_v, attn_norm, w_in,
           b_forget, out_norm_fox, out_norm_sb, w_out, ffn_norm, w_gate, w_up, w_down, final_norm):
    bp, tp, d = x_prompt.shape
    bs, ts, _ = x_sample.shape
    depth = w_in.shape[0]
    hf = cache_fox_k.shape[3]
    hs = cache_sb_k.shape[3]
    w_fox = hf * HEAD_DIM
    w_sb = hs * HEAD_DIM

    xp = x_prompt.reshape(bp * tp, d)
    xs = x_sample.reshape(bs * ts, d)
    outs_p, outs_s = [], []
    for l in range(depth):
        cuts = [0, w_fox, 2 * w_fox, 3 * w_fox, 3 * w_fox + hf, 3 * w_fox + hf + w_sb, 3 * w_fox + hf + 2 * w_sb,
                3 * w_fox + hf + 3 * w_sb]
        wl = jnp.swapaxes(w_in[l], 0, 1)
        w = {
            "attn_norm": attn_norm[l], "ffn_norm": ffn_norm[l],
            "out_norm_fox": out_norm_fox[l], "out_norm_sb": out_norm_sb[l],
            "h_sb": hs, "w_in_t": wl,
            "to_cast_fox": [("w_k_fox", cuts[1], w_fox), ("w_v_fox", cuts[2], w_fox), ("w_q_sb", cuts[4], w_sb)],
            "to_cast_sb": [("w_k_sb", cuts[5], w_sb), ("w_v_sb", cuts[6], w_sb)],
            "w_fl": jnp.pad(wl[cuts[3]:cuts[4], :], ((0, LANES - hf), (0, 0))).astype(BF16),
            "b_fl": jnp.pad(b_forget[l].astype(F32), (0, LANES - hf)).reshape(1, LANES),
            "b_forget": b_forget[l],
            "w_out_f32": w_out[l], "w_gate_f32": w_gate[l], "w_up_f32": w_up[l], "w_down_f32": w_down[l],
        }
        xp, *rest_p = _layer(xp, bp, tp, w, None)
        caches = (cache_fox_k[l], cache_fox_v[l], cache_fox_logf[l], cache_sb_k[l], cache_sb_v[l])
        xs, *rest_s = _layer(xs, bs, ts, w, caches)
        outs_p.append(rest_p)
        outs_s.append(rest_s)

    y_prompt = _rmsnorm(xp, final_norm, F32).reshape(bp, tp, d)
    y_sample = _rmsnorm(xs, final_norm, F32).reshape(bs, ts, d)

    def stack(outs, idx, shape):
        return jnp.stack([o[idx].reshape(shape) for o in outs])

    res = [y_prompt, y_sample]
    for outs, b, t in ((outs_p, bp, tp), (outs_s, bs, ts)):
        res += [stack(outs, 0, (b, t, hf, HEAD_DIM)), stack(outs, 1, (b, t, hf, HEAD_DIM)),
                stack(outs, 2, (b, t, hf)),
                stack(outs, 3, (b, t, hs, HEAD_DIM)), stack(outs, 4, (b, t, hs, HEAD_DIM))]
    return tuple(res)
```
